```python
import jax, jax.numpy as jnp
from jax import lax
import numpy as np

D_MODEL = 1024
BATCH = 16
SEQ = 2048
DEPTH = 2
DEC_BATCH = 32
DEC_SEQ = 8
PAST_LEN = 16384
PAGE_SIZE = 128

GM_WIDTH = 1024
GM_GROUPS = 8
GM_GROUP_CH = GM_WIDTH // GM_GROUPS
CHUNK = 128
HEAD_DIM = 64
HEADS_PER_GROUP = 8
DIL_CONFIGS = ((128, 1), (512, 4), (2048, 16))
N_DIL = 3
ATT_QKV_WIDTH = N_DIL * HEADS_PER_GROUP * HEAD_DIM
ATT_OUT_WIDTH = HEADS_PER_GROUP * HEAD_DIM
BAND = 128
ROPE_THETA = 10000.0
RMS_EPS = 1e-6
LN_EPS = 1e-5
NEG = -1e30

kernel_name = 'dilated_gmlp_hybrid_step'


def _proj_sizes():
    return (GM_WIDTH, GM_WIDTH, GM_WIDTH, ATT_QKV_WIDTH, ATT_QKV_WIDTH, ATT_QKV_WIDTH,
            ATT_OUT_WIDTH, D_MODEL, D_MODEL)


def split_projection(proj):
    parts, start = [], 0
    for size in _proj_sizes():
        parts.append(proj[..., start:start + size])
        start += size
    return parts


def rms_norm(x, g):
    xf = x.astype(jnp.float32)
    y = xf * lax.rsqrt(jnp.mean(xf * xf, axis=-1, keepdims=True) + RMS_EPS)
    return (y * g.astype(jnp.float32)).astype(x.dtype)


def layer_norm(x, g, b):
    xf = x.astype(jnp.float32)
    mu = jnp.mean(xf, axis=-1, keepdims=True)
    var = jnp.mean(jnp.square(xf - mu), axis=-1, keepdims=True)
    y = (xf - mu) * lax.rsqrt(var + LN_EPS) * g.astype(jnp.float32) + b.astype(jnp.float32)
    return y.astype(x.dtype)


def rope(x, pos):
    half = HEAD_DIM // 2
    inv_freq = ROPE_THETA ** (-jnp.arange(half, dtype=jnp.float32) / half)
    ang = pos.astype(jnp.float32)[:, None] * inv_freq[None, :]
    ang = ang.reshape(ang.shape[:1] + (1,) * (x.ndim - 3) + (half,))
    cos, sin = jnp.cos(ang), jnp.sin(ang)
    xf = x.astype(jnp.float32)
    x1, x2 = xf[..., :half], xf[..., half:]
    return jnp.concatenate([x1 * cos - x2 * sin, x2 * cos + x1 * sin], axis=-1).astype(x.dtype)


def dilated_band_attention(q, k, v, dil, n_back):
    B, S, H, hd = q.shape
    n = S // dil
    nb = -(-n // BAND)
    n_pad = nb * BAND

    def to_sub(a):
        return a.reshape(B, n, dil, H, hd).transpose(0, 2, 1, 3, 4)

    qs = jnp.pad(to_sub(q), ((0, 0), (0, 0), (0, n_pad - n), (0, 0), (0, 0)))
    kv_pad = ((0, 0), (0, 0), (BAND, n_pad - n), (0, 0), (0, 0))
    ks = jnp.pad(to_sub(k), kv_pad)
    vs = jnp.pad(to_sub(v), kv_pad)
    qb = qs.reshape(B, dil, nb, BAND, H, hd)

    def band(a):
        prev = a[:, :, :n_pad].reshape(B, dil, nb, BAND, H, hd)
        cur = a[:, :, BAND:].reshape(B, dil, nb, BAND, H, hd)
        return jnp.concatenate([prev, cur], axis=3)

    kb, vb = band(ks), band(vs)
    s = jnp.einsum('brnqhd,brnkhd->brnhqk', qb, kb,
                   preferred_element_type=jnp.float32) * (HEAD_DIM ** -0.5)
    blk = jnp.arange(nb)[:, None, None]
    qi = jnp.arange(BAND)[None, :, None]
    kj = jnp.arange(2 * BAND)[None, None, :]
    dist = qi + BAND - kj
    key_pos = blk * BAND - BAND + kj
    valid = (dist >= 0) & (dist <= n_back) & (key_pos >= 0)
    s = jnp.where(valid[None, None, :, None], s, NEG)
    lse = jax.nn.logsumexp(s, axis=-1)
    p = jnp.exp(s - lse[..., None])
    o = jnp.einsum('brnhqk,brnkhd->brnqhd', p.astype(vb.dtype), vb,
                   preferred_element_type=jnp.float32)
    o = o.reshape(B, dil, n_pad, H, hd)[:, :, :n].transpose(0, 2, 1, 3, 4).reshape(B, S, H, hd)
    lse = lse.transpose(0, 1, 2, 4, 3).reshape(B, dil, n_pad, H)[:, :, :n]
    lse = lse.transpose(0, 2, 1, 3).reshape(B, S, H)
    return o, lse


def dilated_gather_attention(q, k_all, v_all, dil, n_back):
    T = q.shape[1]
    n_prev = k_all.shape[1] - T
    idx = n_prev + jnp.arange(T)[:, None] - dil * jnp.arange(n_back + 1)[None, :]
    valid = idx >= 0
    idx = jnp.maximum(idx, 0)
    kg = jnp.take(k_all, idx, axis=1)
    vg = jnp.take(v_all, idx, axis=1)
    s = jnp.einsum('bthd,btjhd->bhtj', q, kg,
                   preferred_element_type=jnp.float32) * (HEAD_DIM ** -0.5)
    s = jnp.where(valid[None, None], s, NEG)
    lse = jax.nn.logsumexp(s, axis=-1)
    p = jnp.exp(s - lse[..., None])
    o = jnp.einsum('bhtj,btjhd->bthd', p.astype(vg.dtype), vg,
                   preferred_element_type=jnp.float32)
    return o, lse.transpose(0, 2, 1)


def combine_dilations(outs, lses):
    w = jax.nn.softmax(jnp.stack(lses, axis=0), axis=0)
    return jnp.sum(w[..., None] * jnp.stack(outs, axis=0), axis=0)


def gmlp_spatial_prompt(vn, wm, bs):
    B, S, _ = vn.shape
    vc = vn.reshape(B, S // CHUNK, CHUNK, GM_GROUPS, GM_GROUP_CH)
    y = jnp.einsum('gts,bcsgd->bctgd', wm.astype(vn.dtype), vc) + bs.T[:, :, None].astype(vn.dtype)
    return y.reshape(B, S, GM_WIDTH)


def gmlp_spatial_sample(vn, wm, bs):
    B, T, _ = vn.shape
    vc = vn.reshape(B, T, GM_GROUPS, GM_GROUP_CH)
    y = jnp.einsum('gts,bsgd->btgd', wm[:, :T, :T].astype(vn.dtype), vc) + bs.T[:T, :, None].astype(vn.dtype)
    return y.reshape(B, T, GM_WIDTH)


def mixer_layer(x, c, pos, att_core, gm_spatial, w_ada, b_ada, norm_g, w_in,
                gm_ln_g, gm_ln_b, w_gm_out, w_att_out, w_o):
    Bx, S, _ = x.shape
    mod = jnp.dot(jax.nn.silu(c), w_ada) + b_ada
    shift, scale, gate = jnp.split(mod, 3, axis=-1)
    h = rms_norm(x, norm_g) * (1.0 + scale[:, None]) + shift[:, None]
    u, v, z_a, q, k, val, z_b, g_a, g_b = split_projection(jnp.dot(h, w_in))
    vn = layer_norm(jax.nn.gelu(v), gm_ln_g, gm_ln_b)
    y_a = jax.nn.gelu(u) * gm_spatial(vn) * jax.nn.silu(z_a)
    shp = (Bx, S, N_DIL, HEADS_PER_GROUP, HEAD_DIM)
    q = rope(q.reshape(shp), pos)
    k = rope(k.reshape(shp), pos)
    val = val.reshape(shp)
    y_b = att_core(q, k, val).astype(x.dtype).reshape(Bx, S, ATT_OUT_WIDTH) * jax.nn.silu(z_b)
    merged = (jax.nn.sigmoid(g_a) * jnp.dot(y_a, w_gm_out)
              + jax.nn.sigmoid(g_b) * jnp.dot(y_b, w_att_out))
    x = x + gate[:, None] * jnp.dot(merged, w_o)
    return x, k, val, vn


def setup_inputs(seed: int = 0) -> dict:
    key = jax.random.key(seed)
    ks = jax.random.split(key, 24)
    f32 = jnp.float32

    def nrm(k, shape, scale):
        return scale * jax.random.normal(k, shape, f32)

    in_width = sum(_proj_sizes())
    cshape = lambda win: (DEPTH, DEC_BATCH, min(win, PAST_LEN), 2, HEADS_PER_GROUP, HEAD_DIM)
    return {
        'x_prompt': nrm(ks[0], (BATCH, SEQ, D_MODEL), 1.0),
        'x_sample': nrm(ks[1], (DEC_BATCH, DEC_SEQ, D_MODEL), 1.0),
        'cache_kv_w128': nrm(ks[2], cshape(128), 1.0),
        'cache_kv_w512': nrm(ks[3], cshape(512), 1.0),
        'cache_kv_w2048': nrm(ks[4], cshape(2048), 1.0),
        'c_prompt': nrm(ks[5], (BATCH, D_MODEL), 1.0),
        'c_sample': nrm(ks[6], (DEC_BATCH, D_MODEL), 1.0),
        'w_ada': nrm(ks[7], (DEPTH, D_MODEL, 3 * D_MODEL), 0.5 * D_MODEL ** -0.5),
        'b_ada': nrm(ks[8], (DEPTH, 3 * D_MODEL), 0.01),
        'norm_g': 1.0 + nrm(ks[9], (DEPTH, D_MODEL), 0.05),
        'w_in': nrm(ks[10], (DEPTH, D_MODEL, in_width), D_MODEL ** -0.5),
        'gm_ln_g': 1.0 + nrm(ks[11], (DEPTH, GM_WIDTH), 0.05),
        'gm_ln_b': nrm(ks[12], (DEPTH, GM_WIDTH), 0.02),
        'gm_ws': nrm(ks[13], (DEPTH, GM_GROUPS, CHUNK, CHUNK), CHUNK ** -0.5),
        'gm_bs': 1.0 + nrm(ks[14], (DEPTH, GM_GROUPS, CHUNK), 0.1),
        'w_gm_out': nrm(ks[15], (DEPTH, GM_WIDTH, D_MODEL), GM_WIDTH ** -0.5),
        'w_att_out': nrm(ks[16], (DEPTH, ATT_OUT_WIDTH, D_MODEL), ATT_OUT_WIDTH ** -0.5),
        'w_o': nrm(ks[17], (DEPTH, D_MODEL, D_MODEL), D_MODEL ** -0.5),
        'final_g': 1.0 + nrm(ks[18], (D_MODEL,), 0.05),
    }


def reference(x_prompt, x_sample, cache_kv_w128, cache_kv_w512, cache_kv_w2048,
              c_prompt, c_sample, w_ada, b_ada, norm_g, w_in, gm_ln_g, gm_ln_b,
              gm_ws, gm_bs, w_gm_out, w_att_out, w_o, final_g):
    caches = (cache_kv_w128, cache_kv_w512, cache_kv_w2048)
    S = x_prompt.shape[1]
    T = x_sample.shape[1]
    pos_p = jnp.arange(S, dtype=jnp.int32)
    pos_s = PAST_LEN + jnp.arange(T, dtype=jnp.int32)
    causal = jnp.tril(jnp.ones((CHUNK, CHUNK), dtype=bool))
    xp, xs = x_prompt, x_sample
    new_p = [[] for _ in DIL_CONFIGS]
    new_s = [[] for _ in DIL_CONFIGS]
    gm_v_rows = []
    for l in range(DEPTH):
        wm = jnp.where(causal, gm_ws[l], 0.0)
        bs = gm_bs[l]
        lw = (w_ada[l], b_ada[l], norm_g[l], w_in[l], gm_ln_g[l], gm_ln_b[l],
              w_gm_out[l], w_att_out[l], w_o[l])

        def att_prompt(q, k, v):
            outs, lses = [], []
            for g, (win, dil) in enumerate(DIL_CONFIGS):
                o, lse = dilated_band_attention(q[:, :, g], k[:, :, g], v[:, :, g], dil, win // dil)
                outs.append(o)
                lses.append(lse)
            return combine_dilations(outs, lses)

        def att_sample(q, k, v, layer=l):
            outs, lses = [], []
            for g, (win, dil) in enumerate(DIL_CONFIGS):
                buf = caches[g][layer]
                k_all = jnp.concatenate([buf[:, :, 0], k[:, :, g]], axis=1)
                v_all = jnp.concatenate([buf[:, :, 1], v[:, :, g]], axis=1)
                o, lse = dilated_gather_attention(q[:, :, g], k_all, v_all, dil, win // dil)
                outs.append(o)
                lses.append(lse)
            return combine_dilations(outs, lses)

        xp, kp, vp, _ = mixer_layer(xp, c_prompt, pos_p, att_prompt,
                                    lambda vn: gmlp_spatial_prompt(vn, wm, bs), *lw)
        xs, ks_, vs_, vn_s = mixer_layer(xs, c_sample, pos_s, att_sample,
                                         lambda vn: gmlp_spatial_sample(vn, wm, bs), *lw)
        for g, (win, dil) in enumerate(DIL_CONFIGS):
            n_keep = min(win, S)
            new_p[g].append(jnp.stack([kp[:, S - n_keep:, g], vp[:, S - n_keep:, g]], axis=2))
            new_s[g].append(jnp.stack([ks_[:, :, g], vs_[:, :, g]], axis=2))
        gm_v_rows.append(vn_s)

    y_prompt = rms_norm(xp, final_g)
    y_sample = rms_norm(xs, final_g)
    kv_w128_p = jnp.stack(new_p[0])
    kv_w512_p = jnp.stack(new_p[1])
    kv_w2048_p = jnp.stack(new_p[2])
    kv_w128_s = jnp.stack(new_s[0])
    kv_w512_s = jnp.stack(new_s[1])
    kv_w2048_s = jnp.stack(new_s[2])
    gm_v_s = jnp.stack(gm_v_rows)
    return (y_prompt, y_sample, kv_w128_p, kv_w512_p, kv_w2048_p,
            kv_w128_s, kv_w512_s, kv_w2048_s, gm_v_s)
```

```python
import functools

import numpy as np
import jax
import jax.numpy as jnp
from jax import lax
from jax.experimental import pallas as pl
from jax.experimental.pallas import tpu as pltpu

GM_GROUPS = 8
CHUNK = 128
HEAD_DIM = 64
HEADS = 8
DIL_CONFIGS = ((128, 1), (512, 4), (2048, 16))
N_DIL = 3
HW = HEADS * HEAD_DIM
BAND = 128
PAST_LEN = 16384
ROPE_THETA = 10000.0
RMS_EPS = 1e-6
LN_EPS = 1e-5
NEG = -1e30

LANES = 128
TS = 256
VMEM_LIMIT = 52 * 1024 * 1024

BF = jnp.bfloat16
F32 = jnp.float32


def _cp(sem):
    return pltpu.CompilerParams(dimension_semantics=sem, vmem_limit_bytes=VMEM_LIMIT)


def _gelu(x):
    return 0.5 * x * (1.0 + jnp.tanh(0.7978845608028654 * (x + 0.044715 * (x * x * x))))


def _sigmoid(x):
    return 1.0 / (1.0 + jnp.exp(-x))


def _silu(x):
    return x * _sigmoid(x)


def _rms(x, g):
    return x * lax.rsqrt(jnp.mean(x * x, axis=-1, keepdims=True) + RMS_EPS) * g


def _hmod(x, g, mod, d):
    shift = mod[:, 0:d]
    scale = mod[:, d:2 * d]
    return _rms(x, g) * (1.0 + scale) + shift


def _dot(a, b):
    return jnp.dot(a, b, preferred_element_type=F32)


def _dot_nt(a, b):
    return lax.dot_general(a, b, (((1,), (1,)), ((), ())), preferred_element_type=F32)


def _ada_kernel(c_ref, w_ref, b_ref, o_ref):
    c = c_ref[...]
    o_ref[0] = _dot(_silu(c).astype(BF), w_ref[0]) + b_ref[0]


def _ada(c_all, w_ada_bf, b_ada):
    L, D, D3 = w_ada_bf.shape
    n = c_all.shape[0]
    nt = D3 // D
    return pl.pallas_call(
        _ada_kernel,
        grid=(L, nt),
        in_specs=[
            pl.BlockSpec((n, D), lambda l, j: (0, 0)),
            pl.BlockSpec((1, D, D), lambda l, j: (l, 0, j)),
            pl.BlockSpec((1, 1, D), lambda l, j: (l, 0, j)),
        ],
        out_specs=pl.BlockSpec((1, n, D), lambda l, j: (l, 0, j)),
        out_shape=jax.ShapeDtypeStruct((L, n, D3), F32),
        compiler_params=_cp(("arbitrary", "arbitrary")),
        name="ada_mod",
    )(c_all, w_ada_bf, b_ada.reshape(L, 1, D3))


def _branch_a_kernel(x_ref, mod_ref, ng_ref, wa_ref, lng_ref, lnb_ref, wm_ref, bias_ref, wgo_ref, o_ref):
    D = x_ref.shape[-1]
    x = x_ref[0]
    h = _hmod(x, ng_ref[...], mod_ref[0], D).astype(BF)
    pa = _dot(h, wa_ref[...])
    u = pa[:, 0:D]
    v = pa[:, D:2 * D]
    za = pa[:, 2 * D:3 * D]
    ga = pa[:, 3 * D:4 * D]
    gv = _gelu(v)
    mu = jnp.mean(gv, axis=-1, keepdims=True)
    dv = gv - mu
    var = jnp.mean(dv * dv, axis=-1, keepdims=True)
    vn = (dv * lax.rsqrt(var + LN_EPS) * lng_ref[...] + lnb_ref[...]).astype(BF)
    ts = x.shape[0]
    row = lax.broadcasted_iota(jnp.int32, (CHUNK, CHUNK), 0)
    col = lax.broadcasted_iota(jnp.int32, (CHUNK, CHUNK), 1)
    causal = row >= col
    cols = []
    for g in range(GM_GROUPS):
        wmg = jnp.where(causal, wm_ref[g], 0.0).astype(BF)
        rows = []
        for c in range(ts // CHUNK):
            rows.append(_dot(wmg, vn[c * CHUNK:(c + 1) * CHUNK, g * CHUNK:(g + 1) * CHUNK]))
        cols.append(jnp.concatenate(rows, axis=0))
    sp = jnp.concatenate(cols, axis=1)
    bias = jnp.concatenate([bias_ref[...]] * (ts // CHUNK), axis=0)
    ya = (_gelu(u) * (sp + bias) * _silu(za)).astype(BF)
    o_ref[0] = (_sigmoid(ga) * _dot(ya, wgo_ref[...])).astype(o_ref.dtype)


def _branch_a(x, mod, ng, wa, lng, lnb, wm, bias_e, wgo):
    B, S, D = x.shape
    nt = S // TS
    full = lambda *shape: pl.BlockSpec(shape, lambda b, t: (0,) * len(shape))
    return pl.pallas_call(
        _branch_a_kernel,
        grid=(B, nt),
        in_specs=[
            pl.BlockSpec((1, TS, D), lambda b, t: (b, t, 0)),
            pl.BlockSpec((1, 1, 3 * D), lambda b, t: (b, 0, 0)),
            full(1, D), full(D, 4 * D), full(1, D), full(1, D),
            full(GM_GROUPS, CHUNK, CHUNK), full(CHUNK, D), full(D, D),
        ],
        out_specs=pl.BlockSpec((1, TS, D), lambda b, t: (b, t, 0)),
        out_shape=jax.ShapeDtypeStruct((B, S, D), BF),
        compiler_params=_cp(("arbitrary", "arbitrary")),
        name="branch_a",
    )(x, mod, ng, wa, lng, lnb, wm, bias_e, wgo)


def _rope_t(xt, cos, sin):
    nh = xt.shape[0] // HEAD_DIM
    half = HEAD_DIM // 2
    xr = xt.reshape(nh, 2, half, xt.shape[1])
    x1 = xr[:, 0]
    x2 = xr[:, 1]
    o1 = x1 * cos - x2 * sin
    o2 = x2 * cos + x1 * sin
    return jnp.stack([o1, o2], axis=1).reshape(xt.shape)


def _qkv_kernel(keep_tiles, x_ref, mod_ref, ng_ref, wt_ref, cos_ref, sin_ref, p1_ref, p4_ref, p16_ref,
                q0_ref, q1_ref, q2_ref, k0_ref, k1_ref, k2_ref, v0_ref, v1_ref, v2_ref,
                kv0_ref, kv1_ref, kv2_ref):
    D = x_ref.shape[-1]
    t = pl.program_id(1)
    x = x_ref[0]
    h = _hmod(x, ng_ref[...], mod_ref[0], D).astype(BF)
    qkvt = _dot_nt(wt_ref[...], h)
    w = N_DIL * HW
    cos = cos_ref[...][None]
    sin = sin_ref[...][None]
    qt = _rope_t(qkvt[0:w], cos, sin) * (HEAD_DIM ** -0.5)
    kt = _rope_t(qkvt[w:2 * w], cos, sin)
    vt = qkvt[2 * w:3 * w]
    p_refs = (p1_ref, p4_ref, p16_ref)
    q_refs = (q0_ref, q1_ref, q2_ref)
    k_refs = (k0_ref, k1_ref, k2_ref)
    v_refs = (v0_ref, v1_ref, v2_ref)
    kv_refs = (kv0_ref, kv1_ref, kv2_ref)
    for g in range(N_DIL):
        ktg = kt[g * HW:(g + 1) * HW]
        vtg = vt[g * HW:(g + 1) * HW]
        kv_ref = kv_refs[g]
        wcols = kv_ref.shape[-1]

        @pl.when(t >= keep_tiles[g])
        def _():
            kv_ref[0, 0] = ktg[:, TS - wcols:]
            kv_ref[0, 1] = vtg[:, TS - wcols:]

        pt = p_refs[g][...]
        dshape = q_refs[g].shape[1:]
        for src, dst in ((qt[g * HW:(g + 1) * HW], q_refs[g]), (ktg, k_refs[g]), (vtg, v_refs[g])):
            nat = _dot_nt(pt, src.astype(BF)).astype(BF)
            dst[0] = nat.reshape(dshape)


def _perm_t(d):
    m = np.zeros((TS, TS), np.float32)
    per = TS // d
    for r in range(d):
        for i in range(per):
            m[r * per + i, i * d + r] = 1.0
    return jnp.asarray(m, dtype=BF)


def _qkv(x, mod, ng, wt, cos_t, sin_t, keeps):
    B, S, D = x.shape
    nt = S // TS
    full = lambda *shape: pl.BlockSpec(shape, lambda b, t: (0,) * len(shape))
    in_specs = [
        pl.BlockSpec((1, TS, D), lambda b, t: (b, t, 0)),
        pl.BlockSpec((1, 1, 3 * D), lambda b, t: (b, 0, 0)),
        full(1, D), full(3 * N_DIL * HW, D),
        pl.BlockSpec((HEAD_DIM // 2, TS), lambda b, t: (0, t)),
        pl.BlockSpec((HEAD_DIM // 2, TS), lambda b, t: (0, t)),
        full(TS, TS), full(TS, TS), full(TS, TS),
    ]
    out_specs, out_shape = [], []
    for _ in range(3):
        for (_, d) in DIL_CONFIGS:
            out_specs.append(pl.BlockSpec((1, d, TS // d, HW), lambda b, t: (b, 0, t, 0)))
            out_shape.append(jax.ShapeDtypeStruct((B, d, S // d, HW), BF))
    keep_tiles = []
    for keep in keeps:
        wcols = min(keep, TS)
        first = (S - keep) // TS if keep >= TS else nt - 1
        keep_tiles.append(first)
        out_specs.append(pl.BlockSpec((1, 2, HW, wcols),
                                      functools.partial(lambda b, t, f: (b, 0, 0, jnp.maximum(t - f, 0)), f=first)))
        out_shape.append(jax.ShapeDtypeStruct((B, 2, HW, keep), F32))
    return pl.pallas_call(
        functools.partial(_qkv_kernel, tuple(keep_tiles)),
        grid=(B, nt),
        in_specs=in_specs,
        out_specs=out_specs,
        out_shape=out_shape,
        compiler_params=_cp(("arbitrary", "arbitrary")),
        name="qkv_proj",
    )(x, mod, ng, wt, cos_t, sin_t, _perm_t(1), _perm_t(4), _perm_t(16))


def _band_attn_kernel(n, q_ref, k_ref, v_ref, o_ref, l_ref):
    qb = pl.program_id(2)
    win = min(2 * BAND, n)
    if n > win:
        start = pl.multiple_of(jnp.maximum(qb - 1, 0) * BAND, BAND)
        kw = k_ref[0, 0, pl.ds(start, win), :]
        vw = v_ref[0, 0, pl.ds(start, win), :]
    else:
        start = 0
        kw = k_ref[0, 0]
        vw = v_ref[0, 0]
    q = q_ref[0, 0]
    qpos = qb * BAND + lax.broadcasted_iota(jnp.int32, (BAND, win), 0)
    kpos = start + lax.broadcasted_iota(jnp.int32, (BAND, win), 1)
    dist = qpos - kpos
    bias = jnp.where((dist >= 0) & (dist <= BAND), 0.0, NEG)
    lane = lax.broadcasted_iota(jnp.int32, (1, 2 * HEAD_DIM), 1)
    even = lane < HEAD_DIM
    for p in range(HEADS // 2):
        sl = slice(p * 2 * HEAD_DIM, (p + 1) * 2 * HEAD_DIM)
        qp, kp, vp = q[:, sl], kw[:, sl], vw[:, sl]
        acc = None
        ms, ls = [], []
        for sel in (even, ~even):
            s = _dot_nt(jnp.where(sel, qp, jnp.zeros_like(qp)), kp) + bias
            m = jnp.max(s, axis=-1, keepdims=True)
            e = jnp.exp(s - m)
            ls.append(jnp.sum(e, axis=-1, keepdims=True))
            ms.append(m)
            pv = _dot(e.astype(BF), jnp.where(sel, vp, jnp.zeros_like(vp)))
            acc = pv if acc is None else acc + pv
        inv = jnp.where(even, 1.0 / ls[0], 1.0 / ls[1])
        o_ref[0, 0, :, sl] = acc * inv
        l_ref[0, 0, :, sl] = jnp.where(even, ms[0] + jnp.log(ls[0]), ms[1] + jnp.log(ls[1]))


def _band_attn(q, k, v):
    B, d, n, _ = q.shape
    nqb = n // BAND
    return pl.pallas_call(
        functools.partial(_band_attn_kernel, n),
        grid=(B, d, nqb),
        in_specs=[
            pl.BlockSpec((1, 1, BAND, HW), lambda b, r, j: (b, r, j, 0)),
            pl.BlockSpec((1, 1, n, HW), lambda b, r, j: (b, r, 0, 0)),
            pl.BlockSpec((1, 1, n, HW), lambda b, r, j: (b, r, 0, 0)),
        ],
        out_specs=[pl.BlockSpec((1, 1, BAND, HW), lambda b, r, j: (b, r, j, 0))] * 2,
        out_shape=[jax.ShapeDtypeStruct((B, d, n, HW), F32)] * 2,
        compiler_params=_cp(("arbitrary", "arbitrary", "arbitrary")),
        name=f"band_attn_d{d}",
    )(q, k, v)


def _final_kernel(last, x_ref, mod_ref, ng_ref, wzg_ref, ya_ref, o0_ref, l0_ref, o1_ref, l1_ref, o2_ref, l2_ref,
                  wao_ref, wo_ref, fg_ref, out_ref, no_ref, nl_ref):
    D = x_ref.shape[-1]
    x = x_ref[0]
    mod = mod_ref[0]
    h = _hmod(x, ng_ref[...], mod, D).astype(BF)
    nlt = HW // LANES
    outs, lses = [], []
    for g, (o_ref, l_ref) in enumerate(((o0_ref, l0_ref), (o1_ref, l1_ref), (o2_ref, l2_ref))):
        d = o_ref.shape[1]
        per = TS // d
        if d == 1:
            outs.append(o_ref[0, 0])
            lses.append(l_ref[0, 0])
            continue
        for r in range(d):
            ov = o_ref[0, r]
            lv = l_ref[0, r]
            for c in range(nlt):
                no_ref[g * nlt + c, pl.ds(r, per, stride=d), :] = ov[:, c * LANES:(c + 1) * LANES]
                nl_ref[g * nlt + c, pl.ds(r, per, stride=d), :] = lv[:, c * LANES:(c + 1) * LANES]
        outs.append(jnp.concatenate([no_ref[g * nlt + c] for c in range(nlt)], axis=1))
        lses.append(jnp.concatenate([nl_ref[g * nlt + c] for c in range(nlt)], axis=1))
    mx = jnp.maximum(jnp.maximum(lses[0], lses[1]), lses[2])
    ws = [jnp.exp(l - mx) for l in lses]
    den = ws[0] + ws[1] + ws[2]
    att = (ws[0] * outs[0] + ws[1] * outs[1] + ws[2] * outs[2]) / den
    zg = _dot(h, wzg_ref[...])
    zb = zg[:, 0:HW]
    gb = zg[:, HW:HW + D]
    yb = (att * _silu(zb)).astype(BF)
    merged = ya_ref[0].astype(F32) + _sigmoid(gb) * _dot(yb, wao_ref[...])
    gate = mod[:, 2 * D:3 * D]
    xo = x + gate * _dot(merged.astype(BF), wo_ref[...])
    if last:
        xo = _rms(xo, fg_ref[...])
    out_ref[0] = xo


def _final(last, x, mod, ng, wzg, ya, ols, wao, wo, fg):
    B, S, D = x.shape
    nt = S // TS
    full = lambda *shape: pl.BlockSpec(shape, lambda b, t: (0,) * len(shape))
    tile = pl.BlockSpec((1, TS, D), lambda b, t: (b, t, 0))
    in_specs = [tile, pl.BlockSpec((1, 1, 3 * D), lambda b, t: (b, 0, 0)), full(1, D), full(D, HW + D), tile]
    args = [x, mod, ng, wzg, ya]
    for (o, l), (_, d) in zip(ols, DIL_CONFIGS):
        spec = pl.BlockSpec((1, d, TS // d, HW), lambda b, t: (b, 0, t, 0))
        in_specs += [spec, spec]
        args += [o, l]
    in_specs += [full(HW, D), full(D, D), full(1, D)]
    args += [wao, wo, fg]
    return pl.pallas_call(
        functools.partial(_final_kernel, last),
        grid=(B, nt),
        in_specs=in_specs,
        out_specs=tile,
        out_shape=jax.ShapeDtypeStruct((B, S, D), F32),
        scratch_shapes=[pltpu.VMEM((N_DIL * HW // LANES, TS, LANES), F32)] * 2,
        compiler_params=_cp(("arbitrary", "arbitrary")),
        name="merge_out",
    )(*args)


def _sample_proj_kernel(x_ref, mod_ref, ng_ref, w_ref, o_ref):
    D = x_ref.shape[-1]
    h = _hmod(x_ref[...], ng_ref[...], mod_ref[...], D).astype(BF)
    o_ref[...] = _dot(h, w_ref[0])


def _sample_proj(l, x, mod, ng, w_in_bf):
    n, D = x.shape
    N = w_in_bf.shape[-1]
    return pl.pallas_call(
        _sample_proj_kernel,
        grid=(N // D,),
        in_specs=[
            pl.BlockSpec((n, D), lambda j: (0, 0)),
            pl.BlockSpec((n, 3 * D), lambda j: (0, 0)),
            pl.BlockSpec((1, D), lambda j: (0, 0)),
            pl.BlockSpec((1, D, D), lambda j: (l, 0, j)),
        ],
        out_specs=pl.BlockSpec((n, D), lambda j: (0, j)),
        out_shape=jax.ShapeDtypeStruct((n, N), F32),
        compiler_params=_cp(("arbitrary",)),
        name="sample_proj",
    )(x, mod, ng, w_in_bf)


def _rope_nat(x, cos_full, sin_signed):
    lane = lax.broadcasted_iota(jnp.int32, (1, LANES), 1)
    first = (lane & (HEAD_DIM - 1)) < HEAD_DIM // 2
    chunks = []
    for c in range(x.shape[1] // LANES):
        sl = slice(c * LANES, (c + 1) * LANES)
        xc = x[:, sl]
        rot = jnp.where(first, pltpu.roll(xc, LANES - HEAD_DIM // 2, 1), pltpu.roll(xc, HEAD_DIM // 2, 1))
        chunks.append(xc * cos_full[:, sl] + rot * sin_signed[:, sl])
    return jnp.concatenate(chunks, axis=1)


def _sample_mix_kernel(GW, D, p_ref, cos_ref, sin_ref, lng_ref, lnb_ref, coef_ref, bias_ref, c0_ref, c1_ref, c2_ref,
                       ya_ref, yb_ref, sga_ref, sgb_ref, kv0_ref, kv1_ref, kv2_ref, vn_ref):
    T = p_ref.shape[0]
    W3 = N_DIL * HW
    o_q = 3 * GW
    o_zb = o_q + 3 * W3
    o_ga = o_zb + HW
    o_gb = o_ga + D
    u = p_ref[:, 0:GW]
    v = p_ref[:, GW:2 * GW]
    za = p_ref[:, 2 * GW:3 * GW]
    gv = _gelu(v)
    mu = jnp.mean(gv, axis=-1, keepdims=True)
    dv = gv - mu
    var = jnp.mean(dv * dv, axis=-1, keepdims=True)
    vn = dv * lax.rsqrt(var + LN_EPS) * lng_ref[...] + lnb_ref[...]
    vn_ref[...] = vn
    vnb = vn.astype(BF).astype(F32)
    trow = lax.broadcasted_iota(jnp.int32, (T, GW), 0)
    sp = None
    for s in range(T):
        cf = jnp.where(trow >= s, coef_ref[s], 0.0).astype(BF).astype(F32)
        term = cf * vnb[s:s + 1, :]
        sp = term if sp is None else sp + term
    ya_ref[...] = _gelu(u) * (sp + bias_ref[...]) * _silu(za)
    sga_ref[...] = _sigmoid(p_ref[:, o_ga:o_ga + D])
    sgb_ref[...] = _sigmoid(p_ref[:, o_gb:o_gb + D])
    q = _rope_nat(p_ref[:, o_q:o_q + W3], cos_ref[...], sin_ref[...]) * (HEAD_DIM ** -0.5)
    k = _rope_nat(p_ref[:, o_q + W3:o_q + 2 * W3], cos_ref[...], sin_ref[...])
    val = p_ref[:, o_q + 2 * W3:o_q + 3 * W3]
    qb, kb, vb = q.astype(BF), k.astype(BF), val.astype(BF)
    lane = lax.broadcasted_iota(jnp.int32, (1, 2 * HEAD_DIM), 1)
    even = lane < HEAD_DIM
    ms, ls, accs = [], [], []
    for g, ((win, d), c_ref, kv_ref) in enumerate(zip(DIL_CONFIGS, (c0_ref, c1_ref, c2_ref), (kv0_ref, kv1_ref, kv2_ref))):
        n = c_ref.shape[-1]
        kv_ref[:, 0:HW] = k[:, g * HW:(g + 1) * HW]
        kv_ref[:, HW:2 * HW] = val[:, g * HW:(g + 1) * HW]
        tq = lax.broadcasted_iota(jnp.int32, (T, n), 0)
        dist_c = n + tq - lax.broadcasted_iota(jnp.int32, (T, n), 1)
        bias_c = jnp.where((dist_c <= win) & ((dist_c & (d - 1)) == 0), 0.0, NEG)
        dist_n = lax.broadcasted_iota(jnp.int32, (T, T), 0) - lax.broadcasted_iota(jnp.int32, (T, T), 1)
        bias_n = jnp.where((dist_n >= 0) & (dist_n <= win) & ((dist_n & (d - 1)) == 0), 0.0, NEG)
        mg, lg, ag = [], [], []
        for p in range(HEADS // 2):
            sl = slice(g * HW + p * 2 * HEAD_DIM, g * HW + (p + 1) * 2 * HEAD_DIM)
            rows = slice(p * 2 * HEAD_DIM, (p + 1) * 2 * HEAD_DIM)
            ktp = c_ref[0, 0, 0, rows, :].astype(BF)
            vtp = c_ref[0, 0, 1, rows, :].astype(BF)
            qp, knp, vnp = qb[:, sl], kb[:, sl], vb[:, sl]
            pm, plz, pacc = [], [], []
            for sel in (even, ~even):
                qm = jnp.where(sel, qp, jnp.zeros_like(qp))
                s_c = _dot(qm, ktp) + bias_c
                s_n = _dot_nt(qm, knp) + bias_n
                m = jnp.maximum(jnp.max(s_c, axis=-1, keepdims=True), jnp.max(s_n, axis=-1, keepdims=True))
                e_c = jnp.exp(s_c - m)
                e_n = jnp.exp(s_n - m)
                plz.append(jnp.sum(e_c, axis=-1, keepdims=True) + jnp.sum(e_n, axis=-1, keepdims=True))
                pm.append(m)
                pacc.append(_dot_nt(e_c.astype(BF), vtp) + _dot(e_n.astype(BF), vnp))
            mg.append(jnp.where(even, pm[0], pm[1]))
            lg.append(jnp.where(even, plz[0], plz[1]))
            ag.append(jnp.where(even, pacc[0], pacc[1]))
        ms.append(jnp.concatenate(mg, axis=1))
        ls.append(jnp.concatenate(lg, axis=1))
        accs.append(jnp.concatenate(ag, axis=1))
    mx = jnp.maximum(jnp.maximum(ms[0], ms[1]), ms[2])
    ws = [jnp.exp(m - mx) for m in ms]
    num = ws[0] * accs[0] + ws[1] * accs[1] + ws[2] * accs[2]
    den = ws[0] * ls[0] + ws[1] * ls[1] + ws[2] * ls[2]
    yb_ref[...] = (num / den) * _silu(p_ref[:, o_zb:o_zb + HW])


def _sample_mix(l, GW, D, T, proj, cos_s, sin_s, lng, lnb, coef, bias_e, caches):
    n_tok, N = proj.shape
    DB = n_tok // T
    full = lambda *shape: pl.BlockSpec(shape, lambda b: (0,) * len(shape))
    row = lambda w: pl.BlockSpec((T, w), lambda b: (b, 0))
    in_specs = [row(N), full(T, N_DIL * HW), full(T, N_DIL * HW), full(1, GW), full(1, GW), full(T, T, GW),
                full(T, GW)]
    for c in caches:
        in_specs.append(pl.BlockSpec((1, 1, 2, HW, c.shape[-1]), lambda b: (l, b, 0, 0, 0)))
    widths = (GW, HW, D, D, 2 * HW, 2 * HW, 2 * HW, GW)
    return pl.pallas_call(
        functools.partial(_sample_mix_kernel, GW, D),
        grid=(DB,),
        in_specs=in_specs,
        out_specs=[row(w) for w in widths],
        out_shape=[jax.ShapeDtypeStruct((n_tok, w), F32) for w in widths],
        compiler_params=_cp(("arbitrary",)),
        name="sample_mix",
    )(proj, cos_s, sin_s, lng, lnb, coef, bias_e, *caches)


def _sample_out_kernel(last, x_ref, mod_ref, ya_ref, yb_ref, sga_ref, sgb_ref, wgo_ref, wao_ref, wo_ref, fg_ref, o_ref):
    D = x_ref.shape[-1]
    merged = (sga_ref[...] * _dot(ya_ref[...].astype(BF), wgo_ref[...])
              + sgb_ref[...] * _dot(yb_ref[...].astype(BF), wao_ref[...]))
    xo = x_ref[...] + mod_ref[:, 2 * D:3 * D] * _dot(merged.astype(BF), wo_ref[...])
    if last:
        xo = _rms(xo, fg_ref[...])
    o_ref[...] = xo


def _sample_out(last, x, mod, ya, yb, sga, sgb, wgo, wao, wo, fg):
    args = (x, mod, ya, yb, sga, sgb, wgo, wao, wo, fg)
    return pl.pallas_call(
        functools.partial(_sample_out_kernel, last),
        grid=(1,),
        in_specs=[pl.BlockSpec(a.shape, lambda i, nd=a.ndim: (0,) * nd) for a in args],
        out_specs=pl.BlockSpec(x.shape, lambda i: (0, 0)),
        out_shape=jax.ShapeDtypeStruct(x.shape, F32),
        compiler_params=_cp(("arbitrary",)),
        name="sample_out",
    )(*args)


def kernel(x_prompt, x_sample, cache_kv_w128, cache_kv_w512, cache_kv_w2048, c_prompt, c_sample, w_ada, b_ada, norm_g, w_in, gm_ln_g, gm_ln_b, gm_ws, gm_bs, w_gm_out, w_att_out, w_o, final_g):
    B, S, D = x_prompt.shape
    L = w_in.shape[0]
    W3 = N_DIL * HW
    GW = gm_ln_g.shape[-1]
    assert GW == D and S % TS == 0

    c_all = jnp.concatenate([c_prompt, c_sample], axis=0)
    mod_all = _ada(c_all, w_ada.astype(BF), b_ada)

    o_u, o_q, o_zb, o_ga = 0, 3 * GW, 3 * GW + 3 * W3, 3 * GW + 3 * W3 + HW
    o_gb = o_ga + D
    w_in_bf = w_in.astype(BF)
    half = HEAD_DIM // 2
    inv_freq = ROPE_THETA ** (-jnp.arange(half, dtype=F32) / half)
    ang_p = jnp.arange(S, dtype=jnp.int32).astype(F32)[:, None] * inv_freq[None, :]
    cos_p, sin_p = jnp.cos(ang_p).T, jnp.sin(ang_p).T

    DB, T, _ = x_sample.shape
    ang_s = (PAST_LEN + jnp.arange(T, dtype=jnp.int32)).astype(F32)[:, None] * inv_freq[None, :]
    cos_s = jnp.tile(jnp.cos(ang_s), (1, 2 * N_DIL * HEADS))
    sin_s = jnp.tile(jnp.concatenate([-jnp.sin(ang_s), jnp.sin(ang_s)], axis=1), (1, N_DIL * HEADS))
    caches_t = []
    for c in (cache_kv_w128, cache_kv_w512, cache_kv_w2048):
        ct = jnp.transpose(c, (0, 1, 3, 4, 5, 2))
        caches_t.append(ct.reshape(L, DB, 2, HW, c.shape[2]))
    xs = x_sample.reshape(DB * T, D)
    kv_s = [[] for _ in DIL_CONFIGS]
    gm_v = []

    keeps = tuple(min(win, S) for win, _ in DIL_CONFIGS)
    xp = x_prompt
    kv_p = [[] for _ in DIL_CONFIGS]
    for l in range(L):
        mod_p = mod_all[l, :B].reshape(B, 1, 3 * D)
        ng = norm_g[l].reshape(1, D)
        mod_s = jnp.repeat(mod_all[l, B:], T, axis=0)
        proj_s = _sample_proj(l, xs, mod_s, ng, w_in_bf)
        coef = jnp.repeat(jnp.transpose(gm_ws[l][:, :T, :T], (2, 1, 0)), GW // GM_GROUPS, axis=2)
        bias_s = jnp.repeat(gm_bs[l].T[:T], GW // GM_GROUPS, axis=1)
        so = _sample_mix(l, GW, D, T, proj_s, cos_s, sin_s, gm_ln_g[l].reshape(1, GW), gm_ln_b[l].reshape(1, GW),
                         coef, bias_s, caches_t)
        ya_s, yb_s, sga_s, sgb_s = so[0:4]
        xs = _sample_out(l == L - 1, xs, mod_s, ya_s, yb_s, sga_s, sgb_s, w_gm_out[l].astype(BF),
                         w_att_out[l].astype(BF), w_o[l].astype(BF), final_g.reshape(1, D))
        for g in range(N_DIL):
            kv_s[g].append(so[4 + g].reshape(DB, T, 2, HEADS, HEAD_DIM))
        gm_v.append(so[7].reshape(DB, T, GW))
        wa = jnp.concatenate([w_in_bf[l, :, o_u:o_q], w_in_bf[l, :, o_ga:o_gb]], axis=1)
        wqkv_t = w_in_bf[l, :, o_q:o_zb].T
        wzg = jnp.concatenate([w_in_bf[l, :, o_zb:o_ga], w_in_bf[l, :, o_gb:o_gb + D]], axis=1)
        bias_e = jnp.repeat(gm_bs[l].T, GW // GM_GROUPS, axis=1)
        ya = _branch_a(xp, mod_p, ng, wa, gm_ln_g[l].reshape(1, GW), gm_ln_b[l].reshape(1, GW),
                       gm_ws[l], bias_e, w_gm_out[l].astype(BF))
        outs = _qkv(xp, mod_p, ng, wqkv_t, cos_p, sin_p, keeps)
        qs, ks, vs, kvs = outs[0:3], outs[3:6], outs[6:9], outs[9:12]
        ols = [_band_attn(qs[g], ks[g], vs[g]) for g in range(N_DIL)]
        xp = _final(l == L - 1, xp, mod_p, ng, wzg, ya, ols, w_att_out[l].astype(BF), w_o[l].astype(BF),
                    final_g.reshape(1, D))
        for g in range(N_DIL):
            kv_p[g].append(kvs[g])

    def kv_out(parts):
        a = jnp.stack(parts)
        a = a.reshape(a.shape[0], a.shape[1], 2, HEADS, HEAD_DIM, a.shape[-1])
        return jnp.transpose(a, (0, 1, 5, 2, 3, 4))

    return (xp, xs.reshape(DB, T, D), kv_out(kv_p[0]), kv_out(kv_p[1]), kv_out(kv_p[2]),
            jnp.stack(kv_s[0]), jnp.stack(kv_s[1]), jnp.stack(kv_s[2]), jnp.stack(gm_v))
```

```python
import functools

import numpy as np
import jax
import jax.numpy as jnp
from jax import lax
from jax.experimental import pallas as pl
from jax.experimental.pallas import tpu as pltpu

GM_GROUPS = 8
CHUNK = 128
HEAD_DIM = 64
HEADS = 8
DIL_CONFIGS = ((128, 1), (512, 4), (2048, 16))
N_DIL = 3
HW = HEADS * HEAD_DIM
BAND = 128
PAST_LEN = 16384
ROPE_THETA = 10000.0
RMS_EPS = 1e-6
LN_EPS = 1e-5
NEG = -1e30

LANES = 128
TS = 256
TQ = 512
PERM = 256
VMEM_LIMIT = 52 * 1024 * 1024

BF = jnp.bfloat16
F32 = jnp.float32


def _cp(sem, flags=None):
    return pltpu.CompilerParams(dimension_semantics=sem, vmem_limit_bytes=VMEM_LIMIT, flags=flags)


def _gelu(x):
    return 0.5 * x * (1.0 + jnp.tanh(0.7978845608028654 * (x + 0.044715 * (x * x * x))))


def _sigmoid(x):
    return 1.0 / (1.0 + jnp.exp(-x))


def _silu(x):
    return x * _sigmoid(x)


def _rms(x, g):
    return x * lax.rsqrt(jnp.mean(x * x, axis=-1, keepdims=True) + RMS_EPS) * g


def _hmod(x, g, mod, d):
    shift = mod[:, 0:d]
    scale = mod[:, d:2 * d]
    return _rms(x, g) * (1.0 + scale) + shift


def _dot(a, b):
    return jnp.dot(a, b, preferred_element_type=F32)


def _dot_nt(a, b):
    return lax.dot_general(a, b, (((1,), (1,)), ((), ())), preferred_element_type=F32)


def _ada_kernel(c_ref, w_ref, b_ref, o_ref):
    c = c_ref[...]
    o_ref[0] = _dot(_silu(c).astype(BF), w_ref[0]) + b_ref[0]


def _ada(c_all, w_ada_bf, b_ada):
    L, D, D3 = w_ada_bf.shape
    n = c_all.shape[0]
    nt = D3 // D
    return pl.pallas_call(
        _ada_kernel,
        grid=(L, nt),
        in_specs=[
            pl.BlockSpec((n, D), lambda l, j: (0, 0)),
            pl.BlockSpec((1, D, D), lambda l, j: (l, 0, j)),
            pl.BlockSpec((1, 1, D), lambda l, j: (l, 0, j)),
        ],
        out_specs=pl.BlockSpec((1, n, D), lambda l, j: (l, 0, j)),
        out_shape=jax.ShapeDtypeStruct((L, n, D3), F32),
        compiler_params=_cp(("arbitrary", "arbitrary")),
        name="ada_mod",
    )(c_all, w_ada_bf, b_ada.reshape(L, 1, D3))


def _branch_a_kernel(x_ref, mod_ref, ng_ref, wa_ref, lng_ref, lnb_ref, wm_ref, bias_ref, wgo_ref, o_ref):
    D = x_ref.shape[-1]
    x = x_ref[0]
    h = _hmod(x, ng_ref[...], mod_ref[0], D).astype(BF)
    pa = _dot(h, wa_ref[...])
    u = pa[:, 0:D]
    v = pa[:, D:2 * D]
    za = pa[:, 2 * D:3 * D]
    ga = pa[:, 3 * D:4 * D]
    gv = _gelu(v)
    mu = jnp.mean(gv, axis=-1, keepdims=True)
    dv = gv - mu
    var = jnp.mean(dv * dv, axis=-1, keepdims=True)
    vn = (dv * lax.rsqrt(var + LN_EPS) * lng_ref[...] + lnb_ref[...]).astype(BF)
    ts = x.shape[0]
    row = lax.broadcasted_iota(jnp.int32, (CHUNK, CHUNK), 0)
    col = lax.broadcasted_iota(jnp.int32, (CHUNK, CHUNK), 1)
    causal = row >= col
    cols = []
    for g in range(GM_GROUPS):
        wmg = jnp.where(causal, wm_ref[g], 0.0).astype(BF)
        rows = []
        for c in range(ts // CHUNK):
            rows.append(_dot(wmg, vn[c * CHUNK:(c + 1) * CHUNK, g * CHUNK:(g + 1) * CHUNK]))
        cols.append(jnp.concatenate(rows, axis=0))
    sp = jnp.concatenate(cols, axis=1)
    bias = jnp.concatenate([bias_ref[...]] * (ts // CHUNK), axis=0)
    ya = (_gelu(u) * (sp + bias) * _silu(za)).astype(BF)
    o_ref[0] = (_sigmoid(ga) * _dot(ya, wgo_ref[...])).astype(o_ref.dtype)


def _branch_a(x, mod, ng, wa, lng, lnb, wm, bias_e, wgo):
    B, S, D = x.shape
    nt = S // TS
    full = lambda *shape: pl.BlockSpec(shape, lambda b, t: (0,) * len(shape))
    return pl.pallas_call(
        _branch_a_kernel,
        grid=(B, nt),
        in_specs=[
            pl.BlockSpec((1, TS, D), lambda b, t: (b, t, 0)),
            pl.BlockSpec((1, 1, 3 * D), lambda b, t: (b, 0, 0)),
            full(1, D), full(D, 4 * D), full(1, D), full(1, D),
            full(GM_GROUPS, CHUNK, CHUNK), full(CHUNK, D), full(D, D),
        ],
        out_specs=pl.BlockSpec((1, TS, D), lambda b, t: (b, t, 0)),
        out_shape=jax.ShapeDtypeStruct((B, S, D), BF),
        compiler_params=_cp(("arbitrary", "arbitrary")),
        name="branch_a",
    )(x, mod, ng, wa, lng, lnb, wm, bias_e, wgo)


def _rope_t(xt, cos, sin):
    half = HEAD_DIM // 2
    pieces = []
    for h in range(xt.shape[0] // HEAD_DIM):
        x1 = xt[h * HEAD_DIM:h * HEAD_DIM + half]
        x2 = xt[h * HEAD_DIM + half:(h + 1) * HEAD_DIM]
        pieces.append(x1 * cos - x2 * sin)
        pieces.append(x2 * cos + x1 * sin)
    return jnp.concatenate(pieces, axis=0)


def _qkv_kernel(keep_tiles, x_ref, mod_ref, ng_ref, wt_ref, cos_ref, sin_ref, p1_ref, p4_ref, p16_ref,
                q0_ref, q1_ref, q2_ref, k0_ref, k1_ref, k2_ref, v0_ref, v1_ref, v2_ref,
                kv0_ref, kv1_ref, kv2_ref):
    D = x_ref.shape[-1]
    t = pl.program_id(1)
    x = x_ref[0]
    h = _hmod(x, ng_ref[...], mod_ref[0], D).astype(BF)
    w = N_DIL * HW
    cos = cos_ref[...]
    sin = sin_ref[...]
    p_refs = (p1_ref, p4_ref, p16_ref)
    dst_refs = ((q0_ref, q1_ref, q2_ref), (k0_ref, k1_ref, k2_ref), (v0_ref, v1_ref, v2_ref))
    kv_refs = (kv0_ref, kv1_ref, kv2_ref)
    chunks = [(kind, g) for g in range(N_DIL) for kind in range(3)]

    def project(kind, g):
        r0 = kind * w + g * HW
        return _dot_nt(wt_ref[r0:r0 + HW, :], h)

    raw = project(*chunks[0])
    for i, (kind, g) in enumerate(chunks):
        nxt = project(*chunks[i + 1]) if i + 1 < len(chunks) else None
        if kind == 0:
            xt = _rope_t(raw, cos, sin) * (HEAD_DIM ** -0.5)
        elif kind == 1:
            xt = _rope_t(raw, cos, sin)
        else:
            xt = raw
        if kind > 0:
            kv_ref = kv_refs[g]
            wcols = kv_ref.shape[-1]

            @pl.when(t >= keep_tiles[g])
            def _(kv_ref=kv_ref, xt=xt, kind=kind, wcols=wcols):
                kv_ref[0, kind - 1] = xt[:, TQ - wcols:]

        dst = dst_refs[kind][g]
        d = dst.shape[1]
        per = PERM // d
        xb = xt.astype(BF)
        for hf in range(TQ // PERM):
            nat = _dot_nt(p_refs[g][...], xb[:, hf * PERM:(hf + 1) * PERM]).astype(BF)
            dst[0, :, hf * per:(hf + 1) * per, :] = nat.reshape(d, per, HW)
        raw = nxt


def _perm_t(d):
    m = np.zeros((PERM, PERM), np.float32)
    per = PERM // d
    for r in range(d):
        for i in range(per):
            m[r * per + i, i * d + r] = 1.0
    return jnp.asarray(m, dtype=BF)


def _qkv(x, mod, ng, wt, cos_t, sin_t, keeps):
    B, S, D = x.shape
    nt = S // TQ
    full = lambda *shape: pl.BlockSpec(shape, lambda b, t: (0,) * len(shape))
    in_specs = [
        pl.BlockSpec((1, TQ, D), lambda b, t: (b, t, 0)),
        pl.BlockSpec((1, 1, 3 * D), lambda b, t: (b, 0, 0)),
        full(1, D), full(3 * N_DIL * HW, D),
        pl.BlockSpec((HEAD_DIM // 2, TQ), lambda b, t: (0, t)),
        pl.BlockSpec((HEAD_DIM // 2, TQ), lambda b, t: (0, t)),
        full(PERM, PERM), full(PERM, PERM), full(PERM, PERM),
    ]
    out_specs, out_shape = [], []
    for _ in range(3):
        for (_, d) in DIL_CONFIGS:
            out_specs.append(pl.BlockSpec((1, d, TQ // d, HW), lambda b, t: (b, 0, t, 0)))
            out_shape.append(jax.ShapeDtypeStruct((B, d, S // d, HW), BF))
    keep_tiles = []
    for keep in keeps:
        wcols = min(keep, TQ)
        first = (S - keep) // TQ if keep >= TQ else nt - 1
        keep_tiles.append(first)
        out_specs.append(pl.BlockSpec((1, 2, HW, wcols),
                                      functools.partial(lambda b, t, f: (b, 0, 0, jnp.maximum(t - f, 0)), f=first)))
        out_shape.append(jax.ShapeDtypeStruct((B, 2, HW, keep), F32))
    return pl.pallas_call(
        functools.partial(_qkv_kernel, tuple(keep_tiles)),
        grid=(B, nt),
        in_specs=in_specs,
        out_specs=out_specs,
        out_shape=out_shape,
        compiler_params=_cp(("arbitrary", "arbitrary")),
        name="qkv_proj",
    )(x, mod, ng, wt, cos_t, sin_t, _perm_t(1), _perm_t(4), _perm_t(16))


def _band_attn_kernel(n, q_ref, k_ref, v_ref, o_ref, l_ref):
    qb = pl.program_id(2)
    win = min(2 * BAND, n)
    if n > win:
        start = pl.multiple_of(jnp.maximum(qb - 1, 0) * BAND, BAND)
        kw = k_ref[0, 0, pl.ds(start, win), :]
        vw = v_ref[0, 0, pl.ds(start, win), :]
    else:
        start = 0
        kw = k_ref[0, 0]
        vw = v_ref[0, 0]
    q = q_ref[0, 0]
    qpos = qb * BAND + lax.broadcasted_iota(jnp.int32, (BAND, win), 0)
    kpos = start + lax.broadcasted_iota(jnp.int32, (BAND, win), 1)
    dist = qpos - kpos
    bias = jnp.where((dist >= 0) & (dist <= BAND), 0.0, NEG)
    lane = lax.broadcasted_iota(jnp.int32, (1, 2 * HEAD_DIM), 1)
    even = lane < HEAD_DIM
    pairs = [slice(p * 2 * HEAD_DIM, (p + 1) * 2 * HEAD_DIM) for p in range(HEADS // 2)]
    ss = []
    for sl in pairs:
        qp, kp = q[:, sl], kw[:, sl]
        for sel in (even, ~even):
            ss.append(_dot_nt(jnp.where(sel, qp, jnp.zeros_like(qp)), kp) + bias)
    ms = [jnp.max(s, axis=-1, keepdims=True) for s in ss]
    es = [jnp.exp(s - m).astype(BF) for s, m in zip(ss, ms)]
    ones = jnp.ones((win, 2 * HEAD_DIM), BF)
    for p, sl in enumerate(pairs):
        va = jnp.concatenate([vw[:, sl], ones], axis=1)
        r0 = _dot(es[2 * p], va)
        r1 = _dot(es[2 * p + 1], va)
        acc = jnp.where(even, r0[:, :2 * HEAD_DIM], r1[:, :2 * HEAD_DIM])
        den = jnp.where(even, r0[:, 2 * HEAD_DIM:], r1[:, 2 * HEAD_DIM:])
        o_ref[0, 0, :, sl] = acc / den
        l_ref[0, 0, :, sl] = jnp.where(even, ms[2 * p], ms[2 * p + 1]) + jnp.log(den)


def _band_attn(q, k, v):
    B, d, n, _ = q.shape
    nqb = n // BAND
    return pl.pallas_call(
        functools.partial(_band_attn_kernel, n),
        grid=(B, d, nqb),
        in_specs=[
            pl.BlockSpec((1, 1, BAND, HW), lambda b, r, j: (b, r, j, 0)),
            pl.BlockSpec((1, 1, n, HW), lambda b, r, j: (b, r, 0, 0)),
            pl.BlockSpec((1, 1, n, HW), lambda b, r, j: (b, r, 0, 0)),
        ],
        out_specs=[pl.BlockSpec((1, 1, BAND, HW), lambda b, r, j: (b, r, j, 0))] * 2,
        out_shape=[jax.ShapeDtypeStruct((B, d, n, HW), F32)] * 2,
        compiler_params=_cp(("arbitrary", "arbitrary", "arbitrary")),
        name=f"band_attn_d{d}",
    )(q, k, v)


def _final_kernel(last, x_ref, mod_ref, ng_ref, wzg_ref, ya_ref, o0_ref, l0_ref, o1_ref, l1_ref, o2_ref, l2_ref,
                  wao_ref, wo_ref, fg_ref, out_ref, no_ref, nl_ref):
    D = x_ref.shape[-1]
    x = x_ref[0]
    mod = mod_ref[0]
    h = _hmod(x, ng_ref[...], mod, D).astype(BF)
    nlt = HW // LANES
    outs, lses = [], []
    for g, (o_ref, l_ref) in enumerate(((o0_ref, l0_ref), (o1_ref, l1_ref), (o2_ref, l2_ref))):
        d = o_ref.shape[1]
        per = TS // d
        if d == 1:
            outs.append(o_ref[0, 0])
            lses.append(l_ref[0, 0])
            continue
        for r in range(d):
            ov = o_ref[0, r]
            lv = l_ref[0, r]
            for c in range(nlt):
                no_ref[g * nlt + c, pl.ds(r, per, stride=d), :] = ov[:, c * LANES:(c + 1) * LANES]
                nl_ref[g * nlt + c, pl.ds(r, per, stride=d), :] = lv[:, c * LANES:(c + 1) * LANES]
        outs.append(jnp.concatenate([no_ref[g * nlt + c] for c in range(nlt)], axis=1))
        lses.append(jnp.concatenate([nl_ref[g * nlt + c] for c in range(nlt)], axis=1))
    mx = jnp.maximum(jnp.maximum(lses[0], lses[1]), lses[2])
    ws = [jnp.exp(l - mx) for l in lses]
    den = ws[0] + ws[1] + ws[2]
    att = (ws[0] * outs[0] + ws[1] * outs[1] + ws[2] * outs[2]) / den
    zg = _dot(h, wzg_ref[...])
    zb = zg[:, 0:HW]
    gb = zg[:, HW:HW + D]
    yb = (att * _silu(zb)).astype(BF)
    merged = ya_ref[0].astype(F32) + _sigmoid(gb) * _dot(yb, wao_ref[...])
    gate = mod[:, 2 * D:3 * D]
    xo = x + gate * _dot(merged.astype(BF), wo_ref[...])
    if last:
        xo = _rms(xo, fg_ref[...])
    out_ref[0] = xo


def _final(last, x, mod, ng, wzg, ya, ols, wao, wo, fg):
    B, S, D = x.shape
    nt = S // TS
    full = lambda *shape: pl.BlockSpec(shape, lambda b, t: (0,) * len(shape))
    tile = pl.BlockSpec((1, TS, D), lambda b, t: (b, t, 0))
    in_specs = [tile, pl.BlockSpec((1, 1, 3 * D), lambda b, t: (b, 0, 0)), full(1, D), full(D, HW + D), tile]
    args = [x, mod, ng, wzg, ya]
    for (o, l), (_, d) in zip(ols, DIL_CONFIGS):
        spec = pl.BlockSpec((1, d, TS // d, HW), lambda b, t: (b, 0, t, 0))
        in_specs += [spec, spec]
        args += [o, l]
    in_specs += [full(HW, D), full(D, D), full(1, D)]
    args += [wao, wo, fg]
    return pl.pallas_call(
        functools.partial(_final_kernel, last),
        grid=(B, nt),
        in_specs=in_specs,
        out_specs=tile,
        out_shape=jax.ShapeDtypeStruct((B, S, D), F32),
        scratch_shapes=[pltpu.VMEM((N_DIL * HW // LANES, TS, LANES), F32)] * 2,
        compiler_params=_cp(("arbitrary", "arbitrary")),
        name="merge_out",
    )(*args)


def _sample_proj_kernel(x_ref, mod_ref, ng_ref, w_ref, o_ref):
    D = x_ref.shape[-1]
    h = _hmod(x_ref[...], ng_ref[...], mod_ref[...], D).astype(BF)
    o_ref[...] = _dot(h, w_ref[0])


def _sample_proj(l, x, mod, ng, w_in_bf):
    n, D = x.shape
    N = w_in_bf.shape[-1]
    return pl.pallas_call(
        _sample_proj_kernel,
        grid=(N // D,),
        in_specs=[
            pl.BlockSpec((n, D), lambda j: (0, 0)),
            pl.BlockSpec((n, 3 * D), lambda j: (0, 0)),
            pl.BlockSpec((1, D), lambda j: (0, 0)),
            pl.BlockSpec((1, D, D), lambda j: (l, 0, j)),
        ],
        out_specs=pl.BlockSpec((n, D), lambda j: (0, j)),
        out_shape=jax.ShapeDtypeStruct((n, N), F32),
        compiler_params=_cp(("arbitrary",)),
        name="sample_proj",
    )(x, mod, ng, w_in_bf)


def _rope_nat(x, cos_full, sin_signed):
    lane = lax.broadcasted_iota(jnp.int32, (1, LANES), 1)
    first = (lane & (HEAD_DIM - 1)) < HEAD_DIM // 2
    chunks = []
    for c in range(x.shape[1] // LANES):
        sl = slice(c * LANES, (c + 1) * LANES)
        xc = x[:, sl]
        rot = jnp.where(first, pltpu.roll(xc, LANES - HEAD_DIM // 2, 1), pltpu.roll(xc, HEAD_DIM // 2, 1))
        chunks.append(xc * cos_full[:, sl] + rot * sin_signed[:, sl])
    return jnp.concatenate(chunks, axis=1)


def _sample_mix_kernel(GW, D, p_ref, cos_ref, sin_ref, lng_ref, lnb_ref, coef_ref, bias_ref, c0_ref, c1_ref, c2_ref,
                       ya_ref, yb_ref, sga_ref, sgb_ref, kv0_ref, kv1_ref, kv2_ref, vn_ref):
    T = p_ref.shape[0]
    W3 = N_DIL * HW
    o_q = 3 * GW
    o_zb = o_q + 3 * W3
    o_ga = o_zb + HW
    o_gb = o_ga + D
    u = p_ref[:, 0:GW]
    v = p_ref[:, GW:2 * GW]
    za = p_ref[:, 2 * GW:3 * GW]
    gv = _gelu(v)
    mu = jnp.mean(gv, axis=-1, keepdims=True)
    dv = gv - mu
    var = jnp.mean(dv * dv, axis=-1, keepdims=True)
    vn = dv * lax.rsqrt(var + LN_EPS) * lng_ref[...] + lnb_ref[...]
    vn_ref[...] = vn
    vnb = vn.astype(BF).astype(F32)
    trow = lax.broadcasted_iota(jnp.int32, (T, GW), 0)
    sp = None
    for s in range(T):
        cf = jnp.where(trow >= s, coef_ref[s], 0.0).astype(BF).astype(F32)
        term = cf * vnb[s:s + 1, :]
        sp = term if sp is None else sp + term
    ya_ref[...] = _gelu(u) * (sp + bias_ref[...]) * _silu(za)
    sga_ref[...] = _sigmoid(p_ref[:, o_ga:o_ga + D])
    sgb_ref[...] = _sigmoid(p_ref[:, o_gb:o_gb + D])
    q = _rope_nat(p_ref[:, o_q:o_q + W3], cos_ref[...], sin_ref[...]) * (HEAD_DIM ** -0.5)
    k = _rope_nat(p_ref[:, o_q + W3:o_q + 2 * W3], cos_ref[...], sin_ref[...])
    val = p_ref[:, o_q + 2 * W3:o_q + 3 * W3]
    qb, kb, vb = q.astype(BF), k.astype(BF), val.astype(BF)
    lane = lax.broadcasted_iota(jnp.int32, (1, 2 * HEAD_DIM), 1)
    even = lane < HEAD_DIM
    ms, ls, accs = [], [], []
    for g, ((win, d), c_ref, kv_ref) in enumerate(zip(DIL_CONFIGS, (c0_ref, c1_ref, c2_ref), (kv0_ref, kv1_ref, kv2_ref))):
        n = c_ref.shape[-1]
        kv_ref[:, 0:HW] = k[:, g * HW:(g + 1) * HW]
        kv_ref[:, HW:2 * HW] = val[:, g * HW:(g + 1) * HW]
        tq = lax.broadcasted_iota(jnp.int32, (T, n), 0)
        dist_c = n + tq - lax.broadcasted_iota(jnp.int32, (T, n), 1)
        bias_c = jnp.where((dist_c <= win) & ((dist_c & (d - 1)) == 0), 0.0, NEG)
        dist_n = lax.broadcasted_iota(jnp.int32, (T, T), 0) - lax.broadcasted_iota(jnp.int32, (T, T), 1)
        bias_n = jnp.where((dist_n >= 0) & (dist_n <= win) & ((dist_n & (d - 1)) == 0), 0.0, NEG)
        mg, lg, ag = [], [], []
        for p in range(HEADS // 2):
            sl = slice(g * HW + p * 2 * HEAD_DIM, g * HW + (p + 1) * 2 * HEAD_DIM)
            rows = slice(p * 2 * HEAD_DIM, (p + 1) * 2 * HEAD_DIM)
            ktp = c_ref[0, 0, 0, rows, :].astype(BF)
            vtp = c_ref[0, 0, 1, rows, :].astype(BF)
            qp, knp, vnp = qb[:, sl], kb[:, sl], vb[:, sl]
            pm, plz, pacc = [], [], []
            for sel in (even, ~even):
                qm = jnp.where(sel, qp, jnp.zeros_like(qp))
                s_c = _dot(qm, ktp) + bias_c
                s_n = _dot_nt(qm, knp) + bias_n
                m = jnp.maximum(jnp.max(s_c, axis=-1, keepdims=True), jnp.max(s_n, axis=-1, keepdims=True))
                e_c = jnp.exp(s_c - m)
                e_n = jnp.exp(s_n - m)
                plz.append(jnp.sum(e_c, axis=-1, keepdims=True) + jnp.sum(e_n, axis=-1, keepdims=True))
                pm.append(m)
                pacc.append(_dot_nt(e_c.astype(BF), vtp) + _dot(e_n.astype(BF), vnp))
            mg.append(jnp.where(even, pm[0], pm[1]))
            lg.append(jnp.where(even, plz[0], plz[1]))
            ag.append(jnp.where(even, pacc[0], pacc[1]))
        ms.append(jnp.concatenate(mg, axis=1))
        ls.append(jnp.concatenate(lg, axis=1))
        accs.append(jnp.concatenate(ag, axis=1))
    mx = jnp.maximum(jnp.maximum(ms[0], ms[1]), ms[2])
    ws = [jnp.exp(m - mx) for m in ms]
    num = ws[0] * accs[0] + ws[1] * accs[1] + ws[2] * accs[2]
    den = ws[0] * ls[0] + ws[1] * ls[1] + ws[2] * ls[2]
    yb_ref[...] = (num / den) * _silu(p_ref[:, o_zb:o_zb + HW])


def _sample_mix(l, GW, D, T, proj, cos_s, sin_s, lng, lnb, coef, bias_e, caches):
    n_tok, N = proj.shape
    DB = n_tok // T
    full = lambda *shape: pl.BlockSpec(shape, lambda b: (0,) * len(shape))
    row = lambda w: pl.BlockSpec((T, w), lambda b: (b, 0))
    in_specs = [row(N), full(T, N_DIL * HW), full(T, N_DIL * HW), full(1, GW), full(1, GW), full(T, T, GW),
                full(T, GW)]
    for c in caches:
        in_specs.append(pl.BlockSpec((1, 1, 2, HW, c.shape[-1]), lambda b: (l, b, 0, 0, 0)))
    widths = (GW, HW, D, D, 2 * HW, 2 * HW, 2 * HW, GW)
    return pl.pallas_call(
        functools.partial(_sample_mix_kernel, GW, D),
        grid=(DB,),
        in_specs=in_specs,
        out_specs=[row(w) for w in widths],
        out_shape=[jax.ShapeDtypeStruct((n_tok, w), F32) for w in widths],
        compiler_params=_cp(("arbitrary",)),
        name="sample_mix",
    )(proj, cos_s, sin_s, lng, lnb, coef, bias_e, *caches)


def _sample_out_kernel(last, x_ref, mod_ref, ya_ref, yb_ref, sga_ref, sgb_ref, wgo_ref, wao_ref, wo_ref, fg_ref, o_ref):
    D = x_ref.shape[-1]
    merged = (sga_ref[...] * _dot(ya_ref[...].astype(BF), wgo_ref[...])
              + sgb_ref[...] * _dot(yb_ref[...].astype(BF), wao_ref[...]))
    xo = x_ref[...] + mod_ref[:, 2 * D:3 * D] * _dot(merged.astype(BF), wo_ref[...])
    if last:
        xo = _rms(xo, fg_ref[...])
    o_ref[...] = xo


def _sample_out(last, x, mod, ya, yb, sga, sgb, wgo, wao, wo, fg):
    args = (x, mod, ya, yb, sga, sgb, wgo, wao, wo, fg)
    return pl.pallas_call(
        functools.partial(_sample_out_kernel, last),
        grid=(1,),
        in_specs=[pl.BlockSpec(a.shape, lambda i, nd=a.ndim: (0,) * nd) for a in args],
        out_specs=pl.BlockSpec(x.shape, lambda i: (0, 0)),
        out_shape=jax.ShapeDtypeStruct(x.shape, F32),
        compiler_params=_cp(("arbitrary",)),
        name="sample_out",
    )(*args)


def kernel(x_prompt, x_sample, cache_kv_w128, cache_kv_w512, cache_kv_w2048, c_prompt, c_sample, w_ada, b_ada, norm_g, w_in, gm_ln_g, gm_ln_b, gm_ws, gm_bs, w_gm_out, w_att_out, w_o, final_g):
    B, S, D = x_prompt.shape
    L = w_in.shape[0]
    W3 = N_DIL * HW
    GW = gm_ln_g.shape[-1]
    assert GW == D and S % TS == 0

    c_all = jnp.concatenate([c_prompt, c_sample], axis=0)
    mod_all = _ada(c_all, w_ada.astype(BF), b_ada)

    o_u, o_q, o_zb, o_ga = 0, 3 * GW, 3 * GW + 3 * W3, 3 * GW + 3 * W3 + HW
    o_gb = o_ga + D
    w_in_bf = w_in.astype(BF)
    half = HEAD_DIM // 2
    inv_freq = ROPE_THETA ** (-jnp.arange(half, dtype=F32) / half)
    ang_p = jnp.arange(S, dtype=jnp.int32).astype(F32)[:, None] * inv_freq[None, :]
    cos_p, sin_p = jnp.cos(ang_p).T, jnp.sin(ang_p).T

    DB, T, _ = x_sample.shape
    ang_s = (PAST_LEN + jnp.arange(T, dtype=jnp.int32)).astype(F32)[:, None] * inv_freq[None, :]
    cos_s = jnp.tile(jnp.cos(ang_s), (1, 2 * N_DIL * HEADS))
    sin_s = jnp.tile(jnp.concatenate([-jnp.sin(ang_s), jnp.sin(ang_s)], axis=1), (1, N_DIL * HEADS))
    caches_t = []
    for c in (cache_kv_w128, cache_kv_w512, cache_kv_w2048):
        ct = jnp.transpose(c, (0, 1, 3, 4, 5, 2))
        caches_t.append(ct.reshape(L, DB, 2, HW, c.shape[2]))
    xs = x_sample.reshape(DB * T, D)
    kv_s = [[] for _ in DIL_CONFIGS]
    gm_v = []

    keeps = tuple(min(win, S) for win, _ in DIL_CONFIGS)
    xp = x_prompt
    kv_p = [[] for _ in DIL_CONFIGS]
    for l in range(L):
        mod_p = mod_all[l, :B].reshape(B, 1, 3 * D)
        ng = norm_g[l].reshape(1, D)
        mod_s = jnp.repeat(mod_all[l, B:], T, axis=0)
        proj_s = _sample_proj(l, xs, mod_s, ng, w_in_bf)
        coef = jnp.repeat(jnp.transpose(gm_ws[l][:, :T, :T], (2, 1, 0)), GW // GM_GROUPS, axis=2)
        bias_s = jnp.repeat(gm_bs[l].T[:T], GW // GM_GROUPS, axis=1)
        so = _sample_mix(l, GW, D, T, proj_s, cos_s, sin_s, gm_ln_g[l].reshape(1, GW), gm_ln_b[l].reshape(1, GW),
                         coef, bias_s, caches_t)
        ya_s, yb_s, sga_s, sgb_s = so[0:4]
        xs = _sample_out(l == L - 1, xs, mod_s, ya_s, yb_s, sga_s, sgb_s, w_gm_out[l].astype(BF),
                         w_att_out[l].astype(BF), w_o[l].astype(BF), final_g.reshape(1, D))
        for g in range(N_DIL):
            kv_s[g].append(so[4 + g].reshape(DB, T, 2, HEADS, HEAD_DIM))
        gm_v.append(so[7].reshape(DB, T, GW))
        wa = jnp.concatenate([w_in_bf[l, :, o_u:o_q], w_in_bf[l, :, o_ga:o_gb]], axis=1)
        wqkv_t = w_in_bf[l, :, o_q:o_zb].T
        wzg = jnp.concatenate([w_in_bf[l, :, o_zb:o_ga], w_in_bf[l, :, o_gb:o_gb + D]], axis=1)
        bias_e = jnp.repeat(gm_bs[l].T, GW // GM_GROUPS, axis=1)
        ya = _branch_a(xp, mod_p, ng, wa, gm_ln_g[l].reshape(1, GW), gm_ln_b[l].reshape(1, GW),
                       gm_ws[l], bias_e, w_gm_out[l].astype(BF))
        outs = _qkv(xp, mod_p, ng, wqkv_t, cos_p, sin_p, keeps)
        qs, ks, vs, kvs = outs[0:3], outs[3:6], outs[6:9], outs[9:12]
        ols = [_band_attn(qs[g], ks[g], vs[g]) for g in range(N_DIL)]
        xp = _final(l == L - 1, xp, mod_p, ng, wzg, ya, ols, w_att_out[l].astype(BF), w_o[l].astype(BF),
                    final_g.reshape(1, D))
        for g in range(N_DIL):
            kv_p[g].append(kvs[g])

    def kv_out(parts):
        a = jnp.stack(parts)
        a = a.reshape(a.shape[0], a.shape[1], 2, HEADS, HEAD_DIM, a.shape[-1])
        return jnp.transpose(a, (0, 1, 5, 2, 3, 4))

    return (xp, xs.reshape(DB, T, D), kv_out(kv_p[0]), kv_out(kv_p[1]), kv_out(kv_p[2]),
            jnp.stack(kv_s[0]), jnp.stack(kv_s[1]), jnp.stack(kv_s[2]), jnp.stack(gm_v))
```

```python
import functools

import numpy as np
import jax
import jax.numpy as jnp
from jax import lax
from jax.experimental import pallas as pl
from jax.experimental.pallas import tpu as pltpu

GM_GROUPS = 8
CHUNK = 128
HEAD_DIM = 64
HEADS = 8
DIL_CONFIGS = ((128, 1), (512, 4), (2048, 16))
N_DIL = 3
HW = HEADS * HEAD_DIM
BAND = 128
PAST_LEN = 16384
ROPE_THETA = 10000.0
RMS_EPS = 1e-6
LN_EPS = 1e-5
NEG = -1e30

LANES = 128
TS = 512
TQ = 512
PERM = 256
ATT_BLOCKS = 4
VMEM_LIMIT = 52 * 1024 * 1024

BF = jnp.bfloat16
F32 = jnp.float32


def _cp(sem, flags=None):
    return pltpu.CompilerParams(dimension_semantics=sem, vmem_limit_bytes=VMEM_LIMIT, flags=flags)


def _gelu(x):
    return 0.5 * x * (1.0 + jnp.tanh(0.7978845608028654 * (x + 0.044715 * (x * x * x))))


def _sigmoid(x):
    return 1.0 / (1.0 + jnp.exp(-x))


def _silu(x):
    return x * _sigmoid(x)


def _rms(x, g):
    return x * lax.rsqrt(jnp.mean(x * x, axis=-1, keepdims=True) + RMS_EPS) * g


def _hmod(x, g, mod, d):
    shift = mod[:, 0:d]
    scale = mod[:, d:2 * d]
    return _rms(x, g) * (1.0 + scale) + shift


def _dot(a, b):
    return jnp.dot(a, b, preferred_element_type=F32)


def _dot_nt(a, b):
    return lax.dot_general(a, b, (((1,), (1,)), ((), ())), preferred_element_type=F32)


def _ada_kernel(c_ref, w_ref, b_ref, o_ref):
    c = c_ref[...]
    o_ref[0] = _dot(_silu(c).astype(BF), w_ref[0]) + b_ref[0]


def _ada(c_all, w_ada_bf, b_ada):
    L, D, D3 = w_ada_bf.shape
    n = c_all.shape[0]
    nt = D3 // D
    return pl.pallas_call(
        _ada_kernel,
        grid=(L, nt),
        in_specs=[
            pl.BlockSpec((n, D), lambda l, j: (0, 0)),
            pl.BlockSpec((1, D, D), lambda l, j: (l, 0, j)),
            pl.BlockSpec((1, 1, D), lambda l, j: (l, 0, j)),
        ],
        out_specs=pl.BlockSpec((1, n, D), lambda l, j: (l, 0, j)),
        out_shape=jax.ShapeDtypeStruct((L, n, D3), F32),
        compiler_params=_cp(("arbitrary", "arbitrary")),
        name="ada_mod",
    )(c_all, w_ada_bf, b_ada.reshape(L, 1, D3))


def _branch_a_kernel(x_ref, mod_ref, ng_ref, wa_ref, lng_ref, lnb_ref, wm_ref, bias_ref, wgo_ref, o_ref):
    D = x_ref.shape[-1]
    x = x_ref[0]
    h = _hmod(x, ng_ref[...], mod_ref[0], D).astype(BF)
    pa = _dot(h, wa_ref[...])
    u = pa[:, 0:D]
    v = pa[:, D:2 * D]
    za = pa[:, 2 * D:3 * D]
    ga = pa[:, 3 * D:4 * D]
    gv = _gelu(v)
    mu = jnp.mean(gv, axis=-1, keepdims=True)
    dv = gv - mu
    var = jnp.mean(dv * dv, axis=-1, keepdims=True)
    vn = (dv * lax.rsqrt(var + LN_EPS) * lng_ref[...] + lnb_ref[...]).astype(BF)
    ts = x.shape[0]
    row = lax.broadcasted_iota(jnp.int32, (CHUNK, CHUNK), 0)
    col = lax.broadcasted_iota(jnp.int32, (CHUNK, CHUNK), 1)
    causal = row >= col
    cols = []
    for g in range(GM_GROUPS):
        wmg = jnp.where(causal, wm_ref[g], 0.0).astype(BF)
        rows = []
        for c in range(ts // CHUNK):
            rows.append(_dot(wmg, vn[c * CHUNK:(c + 1) * CHUNK, g * CHUNK:(g + 1) * CHUNK]))
        cols.append(jnp.concatenate(rows, axis=0))
    sp = jnp.concatenate(cols, axis=1)
    bias = jnp.concatenate([bias_ref[...]] * (ts // CHUNK), axis=0)
    ya = (_gelu(u) * (sp + bias) * _silu(za)).astype(BF)
    o_ref[0] = (_sigmoid(ga) * _dot(ya, wgo_ref[...])).astype(o_ref.dtype)


def _branch_a(x, mod, ng, wa, lng, lnb, wm, bias_e, wgo):
    B, S, D = x.shape
    nt = S // TS
    full = lambda *shape: pl.BlockSpec(shape, lambda b, t: (0,) * len(shape))
    return pl.pallas_call(
        _branch_a_kernel,
        grid=(B, nt),
        in_specs=[
            pl.BlockSpec((1, TS, D), lambda b, t: (b, t, 0)),
            pl.BlockSpec((1, 1, 3 * D), lambda b, t: (b, 0, 0)),
            full(1, D), full(D, 4 * D), full(1, D), full(1, D),
            full(GM_GROUPS, CHUNK, CHUNK), full(CHUNK, D), full(D, D),
        ],
        out_specs=pl.BlockSpec((1, TS, D), lambda b, t: (b, t, 0)),
        out_shape=jax.ShapeDtypeStruct((B, S, D), BF),
        compiler_params=_cp(("arbitrary", "arbitrary")),
        name="branch_a",
    )(x, mod, ng, wa, lng, lnb, wm, bias_e, wgo)


def _rope_t(xt, cos, sin):
    half = HEAD_DIM // 2
    pieces = []
    for h in range(xt.shape[0] // HEAD_DIM):
        x1 = xt[h * HEAD_DIM:h * HEAD_DIM + half]
        x2 = xt[h * HEAD_DIM + half:(h + 1) * HEAD_DIM]
        pieces.append(x1 * cos - x2 * sin)
        pieces.append(x2 * cos + x1 * sin)
    return jnp.concatenate(pieces, axis=0)


def _qkv_kernel(keep_tiles, n_alias, x_ref, mod_ref, ng_ref, wt_ref, cos_ref, sin_ref, p1_ref, p4_ref, p16_ref, *refs):
    (q0_ref, q1_ref, q2_ref, k0_ref, k1_ref, k2_ref, v0_ref, v1_ref, v2_ref,
     kv0_ref, kv1_ref, kv2_ref) = refs[n_alias:]
    D = x_ref.shape[-1]
    t = pl.program_id(1)
    x = x_ref[0]
    h = _hmod(x, ng_ref[...], mod_ref[0], D).astype(BF)
    w = N_DIL * HW
    cos = cos_ref[...]
    sin = sin_ref[...]
    p_refs = (p1_ref, p4_ref, p16_ref)
    dst_refs = ((q0_ref, q1_ref, q2_ref), (k0_ref, k1_ref, k2_ref), (v0_ref, v1_ref, v2_ref))
    kv_refs = (kv0_ref, kv1_ref, kv2_ref)
    chunks = [(kind, g) for g in range(N_DIL) for kind in range(3)]

    def project(kind, g):
        r0 = kind * w + g * HW
        return _dot_nt(wt_ref[r0:r0 + HW, :], h)

    raw = project(*chunks[0])
    for i, (kind, g) in enumerate(chunks):
        nxt = project(*chunks[i + 1]) if i + 1 < len(chunks) else None
        if kind == 0:
            xt = _rope_t(raw, cos, sin) * (HEAD_DIM ** -0.5)
        elif kind == 1:
            xt = _rope_t(raw, cos, sin)
        else:
            xt = raw
        if kind > 0:
            kv_ref = kv_refs[g]
            wcols = kv_ref.shape[-1]

            @pl.when(t >= keep_tiles[g])
            def _(kv_ref=kv_ref, xt=xt, kind=kind, wcols=wcols):
                kv_ref[0, 0, kind - 1] = xt[:, TQ - wcols:]

        dst = dst_refs[kind][g]
        d = dst.shape[1]
        per = PERM // d
        xb = xt.astype(BF)
        for hf in range(TQ // PERM):
            nat = _dot_nt(p_refs[g][...], xb[:, hf * PERM:(hf + 1) * PERM]).astype(BF)
            dst[0, :, hf * per:(hf + 1) * per, :] = nat.reshape(d, per, HW)
        raw = nxt


def _perm_t(d):
    m = np.zeros((PERM, PERM), np.float32)
    per = PERM // d
    for r in range(d):
        for i in range(per):
            m[r * per + i, i * d + r] = 1.0
    return jnp.asarray(m, dtype=BF)


def _qkv(x, mod, ng, wt, cos_t, sin_t, keeps, layer, n_layers, kv_prev):
    B, S, D = x.shape
    nt = S // TQ
    full = lambda *shape: pl.BlockSpec(shape, lambda b, t: (0,) * len(shape))
    in_specs = [
        pl.BlockSpec((1, TQ, D), lambda b, t: (b, t, 0)),
        pl.BlockSpec((1, 1, 3 * D), lambda b, t: (b, 0, 0)),
        full(1, D), full(3 * N_DIL * HW, D),
        pl.BlockSpec((HEAD_DIM // 2, TQ), lambda b, t: (0, t)),
        pl.BlockSpec((HEAD_DIM // 2, TQ), lambda b, t: (0, t)),
        full(PERM, PERM), full(PERM, PERM), full(PERM, PERM),
    ]
    out_specs, out_shape = [], []
    for _ in range(3):
        for (_, d) in DIL_CONFIGS:
            out_specs.append(pl.BlockSpec((1, d, TQ // d, HW), lambda b, t: (b, 0, t, 0)))
            out_shape.append(jax.ShapeDtypeStruct((B, d, S // d, HW), BF))
    keep_tiles = []
    for keep in keeps:
        wcols = min(keep, TQ)
        first = (S - keep) // TQ if keep >= TQ else nt - 1
        keep_tiles.append(first)
        out_specs.append(pl.BlockSpec(
            (1, 1, 2, HW, wcols),
            functools.partial(lambda b, t, f: (layer, b, 0, 0, jnp.maximum(t - f, 0)), f=first)))
        out_shape.append(jax.ShapeDtypeStruct((n_layers, B, 2, HW, keep), F32))
    args = [x, mod, ng, wt, cos_t, sin_t, _perm_t(1), _perm_t(4), _perm_t(16)]
    aliases = {}
    if kv_prev is not None:
        for i, a in enumerate(kv_prev):
            aliases[len(args)] = 3 * N_DIL + i
            in_specs.append(pl.BlockSpec(memory_space=pl.ANY))
            args.append(a)
    n_alias = len(aliases)
    return pl.pallas_call(
        functools.partial(_qkv_kernel, tuple(keep_tiles), n_alias),
        grid=(B, nt),
        in_specs=in_specs,
        out_specs=out_specs,
        out_shape=out_shape,
        input_output_aliases=aliases,
        compiler_params=_cp(("arbitrary", "arbitrary")),
        name="qkv_proj",
    )(*args)


def _band_attn_block(n, qb, q, k_ref, v_ref, rr):
    win = min(2 * BAND, n)
    if n > win:
        start = pl.multiple_of(jnp.maximum(qb - 1, 0) * BAND, BAND)
        kw = k_ref[0, rr, pl.ds(start, win), :]
        vw = v_ref[0, rr, pl.ds(start, win), :]
    else:
        start = 0
        kw = k_ref[0, rr]
        vw = v_ref[0, rr]
    qpos = qb * BAND + lax.broadcasted_iota(jnp.int32, (BAND, win), 0)
    kpos = start + lax.broadcasted_iota(jnp.int32, (BAND, win), 1)
    dist = qpos - kpos
    bias = jnp.where((dist >= 0) & (dist <= BAND), 0.0, NEG)
    lane = lax.broadcasted_iota(jnp.int32, (1, 2 * HEAD_DIM), 1)
    even = lane < HEAD_DIM
    pairs = [slice(p * 2 * HEAD_DIM, (p + 1) * 2 * HEAD_DIM) for p in range(HEADS // 2)]
    ss = []
    for sl in pairs:
        qp, kp = q[:, sl], kw[:, sl]
        for sel in (even, ~even):
            ss.append(_dot_nt(jnp.where(sel, qp, jnp.zeros_like(qp)), kp) + bias)
    ms = [jnp.max(s, axis=-1, keepdims=True) for s in ss]
    es = [jnp.exp(s - m).astype(BF) for s, m in zip(ss, ms)]
    ones = jnp.ones((win, 2 * HEAD_DIM), BF)
    res = []
    for p, sl in enumerate(pairs):
        va = jnp.concatenate([vw[:, sl], ones], axis=1)
        r0 = _dot(es[2 * p], va)
        r1 = _dot(es[2 * p + 1], va)
        acc = jnp.where(even, r0[:, :2 * HEAD_DIM], r1[:, :2 * HEAD_DIM])
        den = jnp.where(even, r0[:, 2 * HEAD_DIM:], r1[:, 2 * HEAD_DIM:])
        res.append((acc / den, jnp.where(even, ms[2 * p], ms[2 * p + 1]) + jnp.log(den)))
    return pairs, res


def _band_attn_kernel(n, q_ref, k_ref, v_ref, o_ref, l_ref):
    nres = q_ref.shape[1]
    nqb = q_ref.shape[2] // BAND
    step = pl.program_id(2)
    for rr in range(nres):
        for qi in range(nqb):
            rows = slice(qi * BAND, (qi + 1) * BAND)
            pairs, res = _band_attn_block(n, step * nqb + qi, q_ref[0, rr, rows, :], k_ref, v_ref, rr)
            for sl, (o, lse) in zip(pairs, res):
                o_ref[0, rr, rows, sl] = o.astype(o_ref.dtype)
                l_ref[0, rr, rows, sl] = lse


def _band_attn(q, k, v):
    B, d, n, _ = q.shape
    nqb = min(n // BAND, ATT_BLOCKS)
    nres = ATT_BLOCKS // nqb
    return pl.pallas_call(
        functools.partial(_band_attn_kernel, n),
        grid=(B, d // nres, n // (nqb * BAND)),
        in_specs=[
            pl.BlockSpec((1, nres, nqb * BAND, HW), lambda b, r, j: (b, r, j, 0)),
            pl.BlockSpec((1, nres, n, HW), lambda b, r, j: (b, r, 0, 0)),
            pl.BlockSpec((1, nres, n, HW), lambda b, r, j: (b, r, 0, 0)),
        ],
        out_specs=[pl.BlockSpec((1, nres, nqb * BAND, HW), lambda b, r, j: (b, r, j, 0))] * 2,
        out_shape=[jax.ShapeDtypeStruct((B, d, n, HW), BF), jax.ShapeDtypeStruct((B, d, n, HW), F32)],
        compiler_params=_cp(("arbitrary", "arbitrary", "arbitrary")),
        name=f"band_attn_d{d}",
    )(q, k, v)


def _final_kernel(last, x_ref, mod_ref, ng_ref, wzg_ref, ya_ref, o0_ref, l0_ref, o1_ref, l1_ref, o2_ref, l2_ref,
                  wao_ref, wo_ref, fg_ref, out_ref, no_ref, nl_ref):
    D = x_ref.shape[-1]
    x = x_ref[0]
    mod = mod_ref[0]
    h = _hmod(x, ng_ref[...], mod, D).astype(BF)
    nlt = HW // LANES
    outs, lses = [], []
    for g, (o_ref, l_ref) in enumerate(((o0_ref, l0_ref), (o1_ref, l1_ref), (o2_ref, l2_ref))):
        d = o_ref.shape[1]
        per = TS // d
        if d == 1:
            outs.append(o_ref[0, 0].astype(F32))
            lses.append(l_ref[0, 0])
            continue
        for r in range(d):
            ov = o_ref[0, r].astype(F32)
            lv = l_ref[0, r]
            for c in range(nlt):
                no_ref[g * nlt + c, pl.ds(r, per, stride=d), :] = ov[:, c * LANES:(c + 1) * LANES]
                nl_ref[g * nlt + c, pl.ds(r, per, stride=d), :] = lv[:, c * LANES:(c + 1) * LANES]
        outs.append(jnp.concatenate([no_ref[g * nlt + c] for c in range(nlt)], axis=1))
        lses.append(jnp.concatenate([nl_ref[g * nlt + c] for c in range(nlt)], axis=1))
    mx = jnp.maximum(jnp.maximum(lses[0], lses[1]), lses[2])
    ws = [jnp.exp(l - mx) for l in lses]
    den = ws[0] + ws[1] + ws[2]
    att = (ws[0] * outs[0] + ws[1] * outs[1] + ws[2] * outs[2]) / den
    zg = _dot(h, wzg_ref[...])
    zb = zg[:, 0:HW]
    gb = zg[:, HW:HW + D]
    yb = (att * _silu(zb)).astype(BF)
    merged = ya_ref[0].astype(F32) + _sigmoid(gb) * _dot(yb, wao_ref[...])
    gate = mod[:, 2 * D:3 * D]
    xo = x + gate * _dot(merged.astype(BF), wo_ref[...])
    if last:
        xo = _rms(xo, fg_ref[...])
    out_ref[0] = xo


def _final(last, x, mod, ng, wzg, ya, ols, wao, wo, fg):
    B, S, D = x.shape
    nt = S // TS
    full = lambda *shape: pl.BlockSpec(shape, lambda b, t: (0,) * len(shape))
    tile = pl.BlockSpec((1, TS, D), lambda b, t: (b, t, 0))
    in_specs = [tile, pl.BlockSpec((1, 1, 3 * D), lambda b, t: (b, 0, 0)), full(1, D), full(D, HW + D), tile]
    args = [x, mod, ng, wzg, ya]
    for (o, l), (_, d) in zip(ols, DIL_CONFIGS):
        spec = pl.BlockSpec((1, d, TS // d, HW), lambda b, t: (b, 0, t, 0))
        in_specs += [spec, spec]
        args += [o, l]
    in_specs += [full(HW, D), full(D, D), full(1, D)]
    args += [wao, wo, fg]
    return pl.pallas_call(
        functools.partial(_final_kernel, last),
        grid=(B, nt),
        in_specs=in_specs,
        out_specs=tile,
        out_shape=jax.ShapeDtypeStruct((B, S, D), F32),
        scratch_shapes=[pltpu.VMEM((N_DIL * HW // LANES, TS, LANES), F32)] * 2,
        compiler_params=_cp(("arbitrary", "arbitrary")),
        name="merge_out",
    )(*args)


def _sample_proj_kernel(x_ref, mod_ref, ng_ref, w_ref, o_ref):
    D = x_ref.shape[-1]
    h = _hmod(x_ref[...], ng_ref[...], mod_ref[...], D).astype(BF)
    o_ref[...] = _dot(h, w_ref[0])


def _sample_proj(l, x, mod, ng, w_in_bf):
    n, D = x.shape
    N = w_in_bf.shape[-1]
    return pl.pallas_call(
        _sample_proj_kernel,
        grid=(N // D,),
        in_specs=[
            pl.BlockSpec((n, D), lambda j: (0, 0)),
            pl.BlockSpec((n, 3 * D), lambda j: (0, 0)),
            pl.BlockSpec((1, D), lambda j: (0, 0)),
            pl.BlockSpec((1, D, D), lambda j: (l, 0, j)),
        ],
        out_specs=pl.BlockSpec((n, D), lambda j: (0, j)),
        out_shape=jax.ShapeDtypeStruct((n, N), F32),
        compiler_params=_cp(("arbitrary",)),
        name="sample_proj",
    )(x, mod, ng, w_in_bf)


def _rope_nat(x, cos_full, sin_signed):
    lane = lax.broadcasted_iota(jnp.int32, (1, LANES), 1)
    first = (lane & (HEAD_DIM - 1)) < HEAD_DIM // 2
    chunks = []
    for c in range(x.shape[1] // LANES):
        sl = slice(c * LANES, (c + 1) * LANES)
        xc = x[:, sl]
        rot = jnp.where(first, pltpu.roll(xc, LANES - HEAD_DIM // 2, 1), pltpu.roll(xc, HEAD_DIM // 2, 1))
        chunks.append(xc * cos_full[:, sl] + rot * sin_signed[:, sl])
    return jnp.concatenate(chunks, axis=1)


def _sample_mix_kernel(GW, D, p_ref, cos_ref, sin_ref, lng_ref, lnb_ref, coef_ref, bias_ref, c0_ref, c1_ref, c2_ref,
                       ya_ref, yb_ref, sga_ref, sgb_ref, kv0_ref, kv1_ref, kv2_ref, vn_ref):
    T = p_ref.shape[0]
    W3 = N_DIL * HW
    o_q = 3 * GW
    o_zb = o_q + 3 * W3
    o_ga = o_zb + HW
    o_gb = o_ga + D
    u = p_ref[:, 0:GW]
    v = p_ref[:, GW:2 * GW]
    za = p_ref[:, 2 * GW:3 * GW]
    gv = _gelu(v)
    mu = jnp.mean(gv, axis=-1, keepdims=True)
    dv = gv - mu
    var = jnp.mean(dv * dv, axis=-1, keepdims=True)
    vn = dv * lax.rsqrt(var + LN_EPS) * lng_ref[...] + lnb_ref[...]
    vn_ref[...] = vn
    vnb = vn.astype(BF).astype(F32)
    trow = lax.broadcasted_iota(jnp.int32, (T, GW), 0)
    sp = None
    for s in range(T):
        cf = jnp.where(trow >= s, coef_ref[s], 0.0).astype(BF).astype(F32)
        term = cf * vnb[s:s + 1, :]
        sp = term if sp is None else sp + term
    ya_ref[...] = _gelu(u) * (sp + bias_ref[...]) * _silu(za)
    sga_ref[...] = _sigmoid(p_ref[:, o_ga:o_ga + D])
    sgb_ref[...] = _sigmoid(p_ref[:, o_gb:o_gb + D])
    q = _rope_nat(p_ref[:, o_q:o_q + W3], cos_ref[...], sin_ref[...]) * (HEAD_DIM ** -0.5)
    k = _rope_nat(p_ref[:, o_q + W3:o_q + 2 * W3], cos_ref[...], sin_ref[...])
    val = p_ref[:, o_q + 2 * W3:o_q + 3 * W3]
    qb, kb, vb = q.astype(BF), k.astype(BF), val.astype(BF)
    lane = lax.broadcasted_iota(jnp.int32, (1, 2 * HEAD_DIM), 1)
    even = lane < HEAD_DIM
    groups = list(zip(DIL_CONFIGS, (c0_ref, c1_ref, c2_ref), (kv0_ref, kv1_ref, kv2_ref)))
    row_half = jnp.where(lax.broadcasted_iota(jnp.int32, (2 * T, 2 * HEAD_DIM), 0) >= T, 1, 0)
    lane_half = jnp.where(lax.broadcasted_iota(jnp.int32, (2 * T, 2 * HEAD_DIM), 1) >= HEAD_DIM, 1, 0)
    keep2 = row_half == lane_half
    scores = []
    for g, ((win, d), c_ref, kv_ref) in enumerate(groups):
        n = c_ref.shape[-1]
        kv_ref[:, 0:HW] = k[:, g * HW:(g + 1) * HW]
        kv_ref[:, HW:2 * HW] = val[:, g * HW:(g + 1) * HW]
        tq = lax.broadcasted_iota(jnp.int32, (2 * T, n), 0) & (T - 1)
        dist_c = n + tq - lax.broadcasted_iota(jnp.int32, (2 * T, n), 1)
        bias_c = jnp.where((dist_c <= win) & ((dist_c & (d - 1)) == 0), 0.0, NEG)
        dist_n = ((lax.broadcasted_iota(jnp.int32, (2 * T, T), 0) & (T - 1))
                  - lax.broadcasted_iota(jnp.int32, (2 * T, T), 1))
        bias_n = jnp.where((dist_n >= 0) & (dist_n <= win) & ((dist_n & (d - 1)) == 0), 0.0, NEG)
        for p in range(HEADS // 2):
            sl = slice(g * HW + p * 2 * HEAD_DIM, g * HW + (p + 1) * 2 * HEAD_DIM)
            rows = slice(p * 2 * HEAD_DIM, (p + 1) * 2 * HEAD_DIM)
            ktp = c_ref[0, 0, 0, rows, :].astype(BF)
            qp = qb[:, sl]
            q2 = jnp.concatenate([qp, qp], axis=0)
            q2 = jnp.where(keep2, q2, jnp.zeros_like(q2))
            scores.append((_dot(q2, ktp) + bias_c, _dot_nt(q2, kb[:, sl]) + bias_n))
    maxes = [jnp.maximum(jnp.max(s_c, axis=-1, keepdims=True), jnp.max(s_n, axis=-1, keepdims=True))
             for s_c, s_n in scores]
    exps = [(jnp.exp(s_c - m), jnp.exp(s_n - m)) for (s_c, s_n), m in zip(scores, maxes)]
    ms, ls, accs = [], [], []
    i = 0
    for g, ((win, d), c_ref, kv_ref) in enumerate(groups):
        mg, lg, ag = [], [], []
        for p in range(HEADS // 2):
            sl = slice(g * HW + p * 2 * HEAD_DIM, g * HW + (p + 1) * 2 * HEAD_DIM)
            rows = slice(p * 2 * HEAD_DIM, (p + 1) * 2 * HEAD_DIM)
            vtp = c_ref[0, 0, 1, rows, :].astype(BF)
            e_c, e_n = exps[i]
            l2 = jnp.sum(e_c, axis=-1, keepdims=True) + jnp.sum(e_n, axis=-1, keepdims=True)
            acc2 = _dot_nt(e_c.astype(BF), vtp) + _dot(e_n.astype(BF), vb[:, sl])
            mg.append(jnp.where(even, maxes[i][:T], maxes[i][T:]))
            lg.append(jnp.where(even, l2[:T], l2[T:]))
            ag.append(jnp.where(even, acc2[:T], acc2[T:]))
            i += 1
        ms.append(jnp.concatenate(mg, axis=1))
        ls.append(jnp.concatenate(lg, axis=1))
        accs.append(jnp.concatenate(ag, axis=1))
    mx = jnp.maximum(jnp.maximum(ms[0], ms[1]), ms[2])
    ws = [jnp.exp(m - mx) for m in ms]
    num = ws[0] * accs[0] + ws[1] * accs[1] + ws[2] * accs[2]
    den = ws[0] * ls[0] + ws[1] * ls[1] + ws[2] * ls[2]
    yb_ref[...] = (num / den) * _silu(p_ref[:, o_zb:o_zb + HW])


def _sample_mix(l, GW, D, T, proj, cos_s, sin_s, lng, lnb, coef, bias_e, caches):
    n_tok, N = proj.shape
    DB = n_tok // T
    full = lambda *shape: pl.BlockSpec(shape, lambda b: (0,) * len(shape))
    row = lambda w: pl.BlockSpec((T, w), lambda b: (b, 0))
    in_specs = [row(N), full(T, N_DIL * HW), full(T, N_DIL * HW), full(1, GW), full(1, GW), full(T, T, GW),
                full(T, GW)]
    for c in caches:
        in_specs.append(pl.BlockSpec((1, 1, 2, HW, c.shape[-1]), lambda b: (l, b, 0, 0, 0)))
    widths = (GW, HW, D, D, 2 * HW, 2 * HW, 2 * HW, GW)
    return pl.pallas_call(
        functools.partial(_sample_mix_kernel, GW, D),
        grid=(DB,),
        in_specs=in_specs,
        out_specs=[row(w) for w in widths],
        out_shape=[jax.ShapeDtypeStruct((n_tok, w), F32) for w in widths],
        compiler_params=_cp(("arbitrary",)),
        name="sample_mix",
    )(proj, cos_s, sin_s, lng, lnb, coef, bias_e, *caches)


def _sample_out_kernel(last, x_ref, mod_ref, ya_ref, yb_ref, sga_ref, sgb_ref, wgo_ref, wao_ref, wo_ref, fg_ref, o_ref):
    D = x_ref.shape[-1]
    merged = (sga_ref[...] * _dot(ya_ref[...].astype(BF), wgo_ref[...])
              + sgb_ref[...] * _dot(yb_ref[...].astype(BF), wao_ref[...]))
    xo = x_ref[...] + mod_ref[:, 2 * D:3 * D] * _dot(merged.astype(BF), wo_ref[...])
    if last:
        xo = _rms(xo, fg_ref[...])
    o_ref[...] = xo


def _sample_out(last, x, mod, ya, yb, sga, sgb, wgo, wao, wo, fg):
    args = (x, mod, ya, yb, sga, sgb, wgo, wao, wo, fg)
    return pl.pallas_call(
        functools.partial(_sample_out_kernel, last),
        grid=(1,),
        in_specs=[pl.BlockSpec(a.shape, lambda i, nd=a.ndim: (0,) * nd) for a in args],
        out_specs=pl.BlockSpec(x.shape, lambda i: (0, 0)),
        out_shape=jax.ShapeDtypeStruct(x.shape, F32),
        compiler_params=_cp(("arbitrary",)),
        name="sample_out",
    )(*args)


def kernel(x_prompt, x_sample, cache_kv_w128, cache_kv_w512, cache_kv_w2048, c_prompt, c_sample, w_ada, b_ada, norm_g, w_in, gm_ln_g, gm_ln_b, gm_ws, gm_bs, w_gm_out, w_att_out, w_o, final_g):
    B, S, D = x_prompt.shape
    L = w_in.shape[0]
    W3 = N_DIL * HW
    GW = gm_ln_g.shape[-1]
    assert GW == D and S % TS == 0

    c_all = jnp.concatenate([c_prompt, c_sample], axis=0)
    mod_all = _ada(c_all, w_ada.astype(BF), b_ada)

    o_u, o_q, o_zb, o_ga = 0, 3 * GW, 3 * GW + 3 * W3, 3 * GW + 3 * W3 + HW
    o_gb = o_ga + D
    w_in_bf = w_in.astype(BF)
    half = HEAD_DIM // 2
    inv_freq = ROPE_THETA ** (-jnp.arange(half, dtype=F32) / half)
    ang_p = jnp.arange(S, dtype=jnp.int32).astype(F32)[:, None] * inv_freq[None, :]
    cos_p, sin_p = jnp.cos(ang_p).T, jnp.sin(ang_p).T

    DB, T, _ = x_sample.shape
    ang_s = (PAST_LEN + jnp.arange(T, dtype=jnp.int32)).astype(F32)[:, None] * inv_freq[None, :]
    cos_s = jnp.tile(jnp.cos(ang_s), (1, 2 * N_DIL * HEADS))
    sin_s = jnp.tile(jnp.concatenate([-jnp.sin(ang_s), jnp.sin(ang_s)], axis=1), (1, N_DIL * HEADS))
    caches_t = []
    for c in (cache_kv_w128, cache_kv_w512, cache_kv_w2048):
        ct = jnp.transpose(c, (0, 1, 3, 4, 5, 2))
        caches_t.append(ct.reshape(L, DB, 2, HW, c.shape[2]))
    xs = x_sample.reshape(DB * T, D)
    kv_s = [[] for _ in DIL_CONFIGS]
    gm_v = []

    keeps = tuple(min(win, S) for win, _ in DIL_CONFIGS)
    xp = x_prompt
    kv_p = None
    for l in range(L):
        mod_p = mod_all[l, :B].reshape(B, 1, 3 * D)
        ng = norm_g[l].reshape(1, D)
        mod_s = jnp.repeat(mod_all[l, B:], T, axis=0)
        proj_s = _sample_proj(l, xs, mod_s, ng, w_in_bf)
        coef = jnp.repeat(jnp.transpose(gm_ws[l][:, :T, :T], (2, 1, 0)), GW // GM_GROUPS, axis=2)
        bias_s = jnp.repeat(gm_bs[l].T[:T], GW // GM_GROUPS, axis=1)
        so = _sample_mix(l, GW, D, T, proj_s, cos_s, sin_s, gm_ln_g[l].reshape(1, GW), gm_ln_b[l].reshape(1, GW),
                         coef, bias_s, caches_t)
        ya_s, yb_s, sga_s, sgb_s = so[0:4]
        xs = _sample_out(l == L - 1, xs, mod_s, ya_s, yb_s, sga_s, sgb_s, w_gm_out[l].astype(BF),
                         w_att_out[l].astype(BF), w_o[l].astype(BF), final_g.reshape(1, D))
        for g in range(N_DIL):
            kv_s[g].append(so[4 + g].reshape(DB, T, 2, HEADS, HEAD_DIM))
        gm_v.append(so[7].reshape(DB, T, GW))
        wa = jnp.concatenate([w_in_bf[l, :, o_u:o_q], w_in_bf[l, :, o_ga:o_gb]], axis=1)
        wqkv_t = w_in_bf[l, :, o_q:o_zb].T
        wzg = jnp.concatenate([w_in_bf[l, :, o_zb:o_ga], w_in_bf[l, :, o_gb:o_gb + D]], axis=1)
        bias_e = jnp.repeat(gm_bs[l].T, GW // GM_GROUPS, axis=1)
        ya = _branch_a(xp, mod_p, ng, wa, gm_ln_g[l].reshape(1, GW), gm_ln_b[l].reshape(1, GW),
                       gm_ws[l], bias_e, w_gm_out[l].astype(BF))
        outs = _qkv(xp, mod_p, ng, wqkv_t, cos_p, sin_p, keeps, l, L, kv_p)
        qs, ks, vs, kv_p = outs[0:3], outs[3:6], outs[6:9], outs[9:12]
        ols = [_band_attn(qs[g], ks[g], vs[g]) for g in range(N_DIL)]
        xp = _final(l == L - 1, xp, mod_p, ng, wzg, ya, ols, w_att_out[l].astype(BF), w_o[l].astype(BF),
                    final_g.reshape(1, D))

    def kv_out(a):
        a = a.reshape(a.shape[0], a.shape[1], 2, HEADS, HEAD_DIM, a.shape[-1])
        return jnp.transpose(a, (0, 1, 5, 2, 3, 4))

    return (xp, xs.reshape(DB, T, D), kv_out(kv_p[0]), kv_out(kv_p[1]), kv_out(kv_p[2]),
            jnp.stack(kv_s[0]), jnp.stack(kv_s[1]), jnp.stack(kv_s[2]), jnp.stack(gm_v))
```

```python
import functools

import numpy as np
import jax
import jax.numpy as jnp
from jax import lax
from jax.experimental import pallas as pl
from jax.experimental.pallas import tpu as pltpu

GM_GROUPS = 8
CHUNK = 128
HEAD_DIM = 64
HEADS = 8
DIL_CONFIGS = ((128, 1), (512, 4), (2048, 16))
N_DIL = 3
HW = HEADS * HEAD_DIM
BAND = 128
PAST_LEN = 16384
ROPE_THETA = 10000.0
RMS_EPS = 1e-6
LN_EPS = 1e-5
NEG = -1e30

LANES = 128
TS = 512
TQ = 512
PERM = 256
ATT_BLOCKS = 16
MERGE_ROWS = 256
VMEM_LIMIT = 56 * 1024 * 1024

BF = jnp.bfloat16
F32 = jnp.float32


def _cp(sem, flags=None):
    return pltpu.CompilerParams(dimension_semantics=sem, vmem_limit_bytes=VMEM_LIMIT, flags=flags)


def _gelu(x):
    return 0.5 * x * (1.0 + jnp.tanh(0.7978845608028654 * (x + 0.044715 * (x * x * x))))


def _sigmoid(x):
    return 1.0 / (1.0 + jnp.exp(-x))


def _silu(x):
    return x * _sigmoid(x)


def _rms(x, g):
    return x * lax.rsqrt(jnp.mean(x * x, axis=-1, keepdims=True) + RMS_EPS) * g


def _hmod(x, g, mod, d):
    shift = mod[:, 0:d]
    scale = mod[:, d:2 * d]
    return _rms(x, g) * (1.0 + scale) + shift


def _dot(a, b):
    return jnp.dot(a, b, preferred_element_type=F32)


def _dot_nt(a, b):
    return lax.dot_general(a, b, (((1,), (1,)), ((), ())), preferred_element_type=F32)


def _ada_kernel(c_ref, w_ref, b_ref, o_ref):
    c = c_ref[...]
    o_ref[0] = _dot(_silu(c).astype(BF), w_ref[0]) + b_ref[0]


def _ada(c_all, w_ada_bf, b_ada):
    L, D, D3 = w_ada_bf.shape
    n = c_all.shape[0]
    nt = D3 // D
    return pl.pallas_call(
        _ada_kernel,
        grid=(L, nt),
        in_specs=[
            pl.BlockSpec((n, D), lambda l, j: (0, 0)),
            pl.BlockSpec((1, D, D), lambda l, j: (l, 0, j)),
            pl.BlockSpec((1, 1, D), lambda l, j: (l, 0, j)),
        ],
        out_specs=pl.BlockSpec((1, n, D), lambda l, j: (l, 0, j)),
        out_shape=jax.ShapeDtypeStruct((L, n, D3), F32),
        compiler_params=_cp(("arbitrary", "arbitrary")),
        name="ada_mod",
    )(c_all, w_ada_bf, b_ada.reshape(L, 1, D3))


def _branch_a_norm_v(h, wa_ref, lng_ref, lnb_ref):
    D = lng_ref.shape[-1]
    gv = _gelu(_dot(h, wa_ref[:, D:2 * D]))
    mu = jnp.mean(gv, axis=-1, keepdims=True)
    dv = gv - mu
    var = jnp.mean(dv * dv, axis=-1, keepdims=True)
    return (dv * lax.rsqrt(var + LN_EPS) * lng_ref[...] + lnb_ref[...]).astype(BF)


def _branch_a_gates(h, wa_ref):
    D = wa_ref.shape[0]
    return _gelu(_dot(h, wa_ref[:, 0:D])) * _silu(_dot(h, wa_ref[:, 2 * D:3 * D]))


def _branch_a_mix(h, vn, uz, wa_ref, wm_ref, bias_ref, wgo_ref):
    ts = h.shape[0]
    D = wgo_ref.shape[0]
    row = lax.broadcasted_iota(jnp.int32, (CHUNK, CHUNK), 0)
    col = lax.broadcasted_iota(jnp.int32, (CHUNK, CHUNK), 1)
    causal = row >= col
    cols = []
    for g in range(GM_GROUPS):
        wmg = jnp.where(causal, wm_ref[g], 0.0).astype(BF)
        rows = []
        for c in range(ts // CHUNK):
            rows.append(_dot(wmg, vn[c * CHUNK:(c + 1) * CHUNK, g * CHUNK:(g + 1) * CHUNK]))
        cols.append(jnp.concatenate(rows, axis=0))
    sp = jnp.concatenate(cols, axis=1)
    bias = jnp.concatenate([bias_ref[...]] * (ts // CHUNK), axis=0)
    ya = (uz * (sp + bias)).astype(BF)
    ga = _dot(h, wa_ref[:, 3 * D:4 * D])
    return (_sigmoid(ga) * _dot(ya, wgo_ref[...])).astype(BF)


def _rope_t(xt, cos, sin):
    half = HEAD_DIM // 2
    pieces = []
    for h in range(xt.shape[0] // HEAD_DIM):
        x1 = xt[h * HEAD_DIM:h * HEAD_DIM + half]
        x2 = xt[h * HEAD_DIM + half:(h + 1) * HEAD_DIM]
        pieces.append(x1 * cos - x2 * sin)
        pieces.append(x2 * cos + x1 * sin)
    return jnp.concatenate(pieces, axis=0)


def _qkv_kernel(n_alias, x_ref, mod_ref, ng_ref, wt_ref, cos_ref, sin_ref, p1_ref, p4_ref, p16_ref,
                wa_ref, lng_ref, lnb_ref, wm_ref, bias_ref, wgo_ref, *refs):
    (q0_ref, q1_ref, q2_ref, k0_ref, k1_ref, k2_ref, v0_ref, v1_ref, v2_ref,
     kv0_ref, kv1_ref, kv2_ref, ya_ref) = refs[n_alias:]
    D = x_ref.shape[-1]
    h = _hmod(x_ref[0], ng_ref[...], mod_ref[0], D).astype(BF)
    w = N_DIL * HW
    cos = cos_ref[...]
    sin = sin_ref[...]
    p_refs = (p1_ref, p4_ref, p16_ref)
    dst_refs = ((q0_ref, q1_ref, q2_ref), (k0_ref, k1_ref, k2_ref), (v0_ref, v1_ref, v2_ref))
    kv_refs = (kv0_ref, kv1_ref, kv2_ref)
    chunks = [(kind, g) for g in range(N_DIL) for kind in range(3)]

    def project(kind, g):
        r0 = kind * w + g * HW
        return _dot_nt(wt_ref[r0:r0 + HW, :], h)

    vn = _branch_a_norm_v(h, wa_ref, lng_ref, lnb_ref)
    uz = _branch_a_gates(h, wa_ref)
    raw = project(*chunks[0])
    for i, (kind, g) in enumerate(chunks):
        nxt = project(*chunks[i + 1]) if i + 1 < len(chunks) else None
        if kind == 0:
            xt = _rope_t(raw, cos, sin) * (HEAD_DIM ** -0.5)
        elif kind == 1:
            xt = _rope_t(raw, cos, sin)
        else:
            xt = raw
        if kind > 0:
            kv_ref = kv_refs[g]
            kv_ref[0, 0, kind - 1] = xt[:, TQ - kv_ref.shape[-1]:]

        dst = dst_refs[kind][g]
        d = dst.shape[1]
        per = PERM // d
        xb = xt.astype(BF)
        for hf in range(TQ // PERM):
            nat = _dot_nt(p_refs[g][...], xb[:, hf * PERM:(hf + 1) * PERM]).astype(BF)
            dst[0, :, hf * per:(hf + 1) * per, :] = nat.reshape(d, per, HW)
        raw = nxt
    ya_ref[0] = _branch_a_mix(h, vn, uz, wa_ref, wm_ref, bias_ref, wgo_ref)


def _perm_t(d):
    m = np.zeros((PERM, PERM), np.float32)
    per = PERM // d
    for r in range(d):
        for i in range(per):
            m[r * per + i, i * d + r] = 1.0
    return jnp.asarray(m, dtype=BF)


def _qkv(x, mod, ng, wt, cos_t, sin_t, keeps, layer, n_layers, kv_prev, wa, lng, lnb, wm, bias_e, wgo):
    B, S, D = x.shape
    nt = S // TQ
    full = lambda *shape: pl.BlockSpec(shape, lambda b, t: (0,) * len(shape))
    in_specs = [
        pl.BlockSpec((1, TQ, D), lambda b, t: (b, t, 0)),
        pl.BlockSpec((1, 1, 3 * D), lambda b, t: (b, 0, 0)),
        full(1, D), full(3 * N_DIL * HW, D),
        pl.BlockSpec((HEAD_DIM // 2, TQ), lambda b, t: (0, t)),
        pl.BlockSpec((HEAD_DIM // 2, TQ), lambda b, t: (0, t)),
        full(PERM, PERM), full(PERM, PERM), full(PERM, PERM),
        full(D, 4 * D), full(1, D), full(1, D), full(GM_GROUPS, CHUNK, CHUNK), full(CHUNK, D), full(D, D),
    ]
    out_specs, out_shape = [], []
    for _ in range(3):
        for (_, d) in DIL_CONFIGS:
            out_specs.append(pl.BlockSpec((1, d, TQ // d, HW), lambda b, t: (b, 0, t, 0)))
            out_shape.append(jax.ShapeDtypeStruct((B, d, S // d, HW), BF))
    for keep in keeps:
        wcols = min(keep, TQ)
        first = (S - keep) // TQ if keep >= TQ else nt - 1
        out_specs.append(pl.BlockSpec(
            (1, 1, 2, HW, wcols),
            functools.partial(lambda b, t, f: (layer, b, 0, 0, jnp.maximum(t - f, 0)), f=first)))
        out_shape.append(jax.ShapeDtypeStruct((n_layers, B, 2, HW, keep), F32))
    out_specs.append(pl.BlockSpec((1, TQ, D), lambda b, t: (b, t, 0)))
    out_shape.append(jax.ShapeDtypeStruct((B, S, D), BF))
    args = [x, mod, ng, wt, cos_t, sin_t, _perm_t(1), _perm_t(4), _perm_t(16), wa, lng, lnb, wm, bias_e, wgo]
    aliases = {}
    if kv_prev is not None:
        for i, a in enumerate(kv_prev):
            aliases[len(args)] = 3 * N_DIL + i
            in_specs.append(pl.BlockSpec(memory_space=pl.ANY))
            args.append(a)
    n_alias = len(aliases)
    return pl.pallas_call(
        functools.partial(_qkv_kernel, n_alias),
        grid=(B, nt),
        in_specs=in_specs,
        out_specs=out_specs,
        out_shape=out_shape,
        input_output_aliases=aliases,
        compiler_params=_cp(("arbitrary", "arbitrary")),
        name="proj",
    )(*args)


def _band_attn_block(n, qb, q, k_ref, v_ref, rr):
    win = min(2 * BAND, n)
    if n > win:
        start = pl.multiple_of(jnp.maximum(qb - 1, 0) * BAND, BAND)
        kw = k_ref[0, rr, pl.ds(start, win), :]
        vw = v_ref[0, rr, pl.ds(start, win), :]
    else:
        start = 0
        kw = k_ref[0, rr]
        vw = v_ref[0, rr]
    qpos = qb * BAND + lax.broadcasted_iota(jnp.int32, (BAND, win), 0)
    kpos = start + lax.broadcasted_iota(jnp.int32, (BAND, win), 1)
    dist = qpos - kpos
    bias = jnp.where((dist >= 0) & (dist <= BAND), 0.0, NEG)
    lane = lax.broadcasted_iota(jnp.int32, (1, 2 * HEAD_DIM), 1)
    even = lane < HEAD_DIM
    pairs = [slice(p * 2 * HEAD_DIM, (p + 1) * 2 * HEAD_DIM) for p in range(HEADS // 2)]
    ss = []
    for sl in pairs:
        qp, kp = q[:, sl], kw[:, sl]
        for sel in (even, ~even):
            ss.append(_dot_nt(jnp.where(sel, qp, jnp.zeros_like(qp)), kp) + bias)
    ms = [jnp.max(s, axis=-1, keepdims=True) for s in ss]
    es = [jnp.exp(s - m).astype(BF) for s, m in zip(ss, ms)]
    ones = jnp.ones((win, 2 * HEAD_DIM), BF)
    res = []
    for p, sl in enumerate(pairs):
        va = jnp.concatenate([vw[:, sl], ones], axis=1)
        r0 = _dot(es[2 * p], va)
        r1 = _dot(es[2 * p + 1], va)
        acc = jnp.where(even, r0[:, :2 * HEAD_DIM], r1[:, :2 * HEAD_DIM])
        den = jnp.where(even, r0[:, 2 * HEAD_DIM:], r1[:, 2 * HEAD_DIM:])
        res.append((acc / den, jnp.where(even, ms[2 * p], ms[2 * p + 1]) + jnp.log(den)))
    return pairs, res


def _band_attn_kernel(n, q_ref, k_ref, v_ref, o_ref, l_ref):
    nres = q_ref.shape[1]
    nqb = q_ref.shape[2] // BAND
    step = pl.program_id(2)
    for rr in range(nres):
        for qi in range(nqb):
            rows = slice(qi * BAND, (qi + 1) * BAND)
            pairs, res = _band_attn_block(n, step * nqb + qi, q_ref[0, rr, rows, :], k_ref, v_ref, rr)
            for sl, (o, lse) in zip(pairs, res):
                o_ref[0, rr, rows, sl] = o.astype(o_ref.dtype)
                l_ref[0, rr, rows, sl] = lse


def _band_attn(q, k, v):
    B, d, n, _ = q.shape
    nqb = min(n // BAND, ATT_BLOCKS)
    nres = ATT_BLOCKS // nqb
    return pl.pallas_call(
        functools.partial(_band_attn_kernel, n),
        grid=(B, d // nres, n // (nqb * BAND)),
        in_specs=[
            pl.BlockSpec((1, nres, nqb * BAND, HW), lambda b, r, j: (b, r, j, 0)),
            pl.BlockSpec((1, nres, n, HW), lambda b, r, j: (b, r, 0, 0)),
            pl.BlockSpec((1, nres, n, HW), lambda b, r, j: (b, r, 0, 0)),
        ],
        out_specs=[pl.BlockSpec((1, nres, nqb * BAND, HW), lambda b, r, j: (b, r, j, 0))] * 2,
        out_shape=[jax.ShapeDtypeStruct((B, d, n, HW), BF), jax.ShapeDtypeStruct((B, d, n, HW), F32)],
        compiler_params=_cp(("arbitrary", "arbitrary", "arbitrary")),
        name=f"band_attn_d{d}",
    )(q, k, v)


def _final_kernel(last, x_ref, mod_ref, ng_ref, wzg_ref, ya_ref, o0_ref, l0_ref, o1_ref, l1_ref, o2_ref, l2_ref,
                  wao_ref, wo_ref, fg_ref, out_ref, no_ref, nl_ref):
    D = x_ref.shape[-1]
    mod = mod_ref[0]
    gate = mod[:, 2 * D:3 * D]
    nlt = HW // LANES
    nsub = TS // MERGE_ROWS
    subs = [slice(i * MERGE_ROWS, (i + 1) * MERGE_ROWS) for i in range(nsub)]

    def attention_rows(i):
        outs, lses = [], []
        for g, (o_ref, l_ref) in enumerate(((o0_ref, l0_ref), (o1_ref, l1_ref), (o2_ref, l2_ref))):
            d = o_ref.shape[1]
            per = MERGE_ROWS // d
            src = slice(i * per, (i + 1) * per)
            if d == 1:
                outs.append(o_ref[0, 0, src, :].astype(F32))
                lses.append(l_ref[0, 0, src, :])
                continue
            for r in range(d):
                ov = o_ref[0, r, src, :].astype(F32)
                lv = l_ref[0, r, src, :]
                for c in range(nlt):
                    dst = pl.ds(i * MERGE_ROWS + r, per, stride=d)
                    no_ref[g * nlt + c, dst, :] = ov[:, c * LANES:(c + 1) * LANES]
                    nl_ref[g * nlt + c, dst, :] = lv[:, c * LANES:(c + 1) * LANES]
            outs.append(jnp.concatenate([no_ref[g * nlt + c, subs[i], :] for c in range(nlt)], axis=1))
            lses.append(jnp.concatenate([nl_ref[g * nlt + c, subs[i], :] for c in range(nlt)], axis=1))
        mx = jnp.maximum(jnp.maximum(lses[0], lses[1]), lses[2])
        ws = [jnp.exp(l - mx) for l in lses]
        den = ws[0] + ws[1] + ws[2]
        return (ws[0] * outs[0] + ws[1] * outs[1] + ws[2] * outs[2]) / den

    hs = [_hmod(x_ref[0, sl, :], ng_ref[...], mod, D).astype(BF) for sl in subs]
    zgs = [_dot(h, wzg_ref[...]) for h in hs]
    atts = [attention_rows(i) for i in range(nsub)]
    mbs = []
    for i in range(nsub):
        yb = (atts[i] * _silu(zgs[i][:, 0:HW])).astype(BF)
        mbs.append(_dot(yb, wao_ref[...]))
    for i, sl in enumerate(subs):
        merged = ya_ref[0, sl, :].astype(F32) + _sigmoid(zgs[i][:, HW:HW + D]) * mbs[i]
        xo = x_ref[0, sl, :] + gate * _dot(merged.astype(BF), wo_ref[...])
        if last:
            xo = _rms(xo, fg_ref[...])
        out_ref[0, sl, :] = xo


def _final(last, x, mod, ng, wzg, ya, ols, wao, wo, fg):
    B, S, D = x.shape
    nt = S // TS
    full = lambda *shape: pl.BlockSpec(shape, lambda b, t: (0,) * len(shape))
    tile = pl.BlockSpec((1, TS, D), lambda b, t: (b, t, 0))
    in_specs = [tile, pl.BlockSpec((1, 1, 3 * D), lambda b, t: (b, 0, 0)), full(1, D), full(D, HW + D), tile]
    args = [x, mod, ng, wzg, ya]
    for (o, l), (_, d) in zip(ols, DIL_CONFIGS):
        spec = pl.BlockSpec((1, d, TS // d, HW), lambda b, t: (b, 0, t, 0))
        in_specs += [spec, spec]
        args += [o, l]
    in_specs += [full(HW, D), full(D, D), full(1, D)]
    args += [wao, wo, fg]
    return pl.pallas_call(
        functools.partial(_final_kernel, last),
        grid=(B, nt),
        in_specs=in_specs,
        out_specs=tile,
        out_shape=jax.ShapeDtypeStruct((B, S, D), F32),
        scratch_shapes=[pltpu.VMEM((N_DIL * HW // LANES, TS, LANES), F32)] * 2,
        compiler_params=_cp(("arbitrary", "arbitrary")),
        name="merge_out",
    )(*args)


def _sample_proj_kernel(x_ref, mod_ref, ng_ref, w_ref, o_ref):
    D = x_ref.shape[-1]
    h = _hmod(x_ref[...], ng_ref[...], mod_ref[...], D).astype(BF)
    o_ref[...] = _dot(h, w_ref[0])


def _sample_proj(l, x, mod, ng, w_in_bf):
    n, D = x.shape
    N = w_in_bf.shape[-1]
    return pl.pallas_call(
        _sample_proj_kernel,
        grid=(N // D,),
        in_specs=[
            pl.BlockSpec((n, D), lambda j: (0, 0)),
            pl.BlockSpec((n, 3 * D), lambda j: (0, 0)),
            pl.BlockSpec((1, D), lambda j: (0, 0)),
            pl.BlockSpec((1, D, D), lambda j: (l, 0, j)),
        ],
        out_specs=pl.BlockSpec((n, D), lambda j: (0, j)),
        out_shape=jax.ShapeDtypeStruct((n, N), F32),
        compiler_params=_cp(("arbitrary",)),
        name="sample_proj",
    )(x, mod, ng, w_in_bf)


def _rope_nat(x, cos_full, sin_signed):
    lane = lax.broadcasted_iota(jnp.int32, (1, LANES), 1)
    first = (lane & (HEAD_DIM - 1)) < HEAD_DIM // 2
    chunks = []
    for c in range(x.shape[1] // LANES):
        sl = slice(c * LANES, (c + 1) * LANES)
        xc = x[:, sl]
        rot = jnp.where(first, pltpu.roll(xc, LANES - HEAD_DIM // 2, 1), pltpu.roll(xc, HEAD_DIM // 2, 1))
        chunks.append(xc * cos_full[:, sl] + rot * sin_signed[:, sl])
    return jnp.concatenate(chunks, axis=1)


def _sample_mix_kernel(GW, D, p_ref, cos_ref, sin_ref, lng_ref, lnb_ref, coef_ref, bias_ref, c0_ref, c1_ref, c2_ref,
                       ya_ref, yb_ref, sga_ref, sgb_ref, kv0_ref, kv1_ref, kv2_ref, vn_ref):
    T = p_ref.shape[0]
    W3 = N_DIL * HW
    o_q = 3 * GW
    o_zb = o_q + 3 * W3
    o_ga = o_zb + HW
    o_gb = o_ga + D
    u = p_ref[:, 0:GW]
    v = p_ref[:, GW:2 * GW]
    za = p_ref[:, 2 * GW:3 * GW]
    gv = _gelu(v)
    mu = jnp.mean(gv, axis=-1, keepdims=True)
    dv = gv - mu
    var = jnp.mean(dv * dv, axis=-1, keepdims=True)
    vn = dv * lax.rsqrt(var + LN_EPS) * lng_ref[...] + lnb_ref[...]
    vn_ref[...] = vn
    vnb = vn.astype(BF).astype(F32)
    trow = lax.broadcasted_iota(jnp.int32, (T, GW), 0)
    sp = None
    for s in range(T):
        cf = jnp.where(trow >= s, coef_ref[s], 0.0).astype(BF).astype(F32)
        term = cf * vnb[s:s + 1, :]
        sp = term if sp is None else sp + term
    ya_ref[...] = _gelu(u) * (sp + bias_ref[...]) * _silu(za)
    sga_ref[...] = _sigmoid(p_ref[:, o_ga:o_ga + D])
    sgb_ref[...] = _sigmoid(p_ref[:, o_gb:o_gb + D])
    q = _rope_nat(p_ref[:, o_q:o_q + W3], cos_ref[...], sin_ref[...]) * (HEAD_DIM ** -0.5)
    k = _rope_nat(p_ref[:, o_q + W3:o_q + 2 * W3], cos_ref[...], sin_ref[...])
    val = p_ref[:, o_q + 2 * W3:o_q + 3 * W3]
    qb, kb, vb = q.astype(BF), k.astype(BF), val.astype(BF)
    lane = lax.broadcasted_iota(jnp.int32, (1, 2 * HEAD_DIM), 1)
    even = lane < HEAD_DIM
    groups = list(zip(DIL_CONFIGS, (c0_ref, c1_ref, c2_ref), (kv0_ref, kv1_ref, kv2_ref)))
    row_half = jnp.where(lax.broadcasted_iota(jnp.int32, (2 * T, 2 * HEAD_DIM), 0) >= T, 1, 0)
    lane_half = jnp.where(lax.broadcasted_iota(jnp.int32, (2 * T, 2 * HEAD_DIM), 1) >= HEAD_DIM, 1, 0)
    keep2 = row_half == lane_half
    scores = []
    for g, ((win, d), c_ref, kv_ref) in enumerate(groups):
        n = c_ref.shape[-1]
        kv_ref[:, 0:HW] = k[:, g * HW:(g + 1) * HW]
        kv_ref[:, HW:2 * HW] = val[:, g * HW:(g + 1) * HW]
        tq = lax.broadcasted_iota(jnp.int32, (2 * T, n), 0) & (T - 1)
        dist_c = n + tq - lax.broadcasted_iota(jnp.int32, (2 * T, n), 1)
        bias_c = jnp.where((dist_c <= win) & ((dist_c & (d - 1)) == 0), 0.0, NEG)
        dist_n = ((lax.broadcasted_iota(jnp.int32, (2 * T, T), 0) & (T - 1))
                  - lax.broadcasted_iota(jnp.int32, (2 * T, T), 1))
        bias_n = jnp.where((dist_n >= 0) & (dist_n <= win) & ((dist_n & (d - 1)) == 0), 0.0, NEG)
        for p in range(HEADS // 2):
            sl = slice(g * HW + p * 2 * HEAD_DIM, g * HW + (p + 1) * 2 * HEAD_DIM)
            rows = slice(p * 2 * HEAD_DIM, (p + 1) * 2 * HEAD_DIM)
            ktp = c_ref[0, 0, 0, rows, :].astype(BF)
            qp = qb[:, sl]
            q2 = jnp.concatenate([qp, qp], axis=0)
            q2 = jnp.where(keep2, q2, jnp.zeros_like(q2))
            scores.append((_dot(q2, ktp) + bias_c, _dot_nt(q2, kb[:, sl]) + bias_n))
    maxes = [jnp.maximum(jnp.max(s_c, axis=-1, keepdims=True), jnp.max(s_n, axis=-1, keepdims=True))
             for s_c, s_n in scores]
    exps = [(jnp.exp(s_c - m), jnp.exp(s_n - m)) for (s_c, s_n), m in zip(scores, maxes)]
    ms, ls, accs = [], [], []
    i = 0
    for g, ((win, d), c_ref, kv_ref) in enumerate(groups):
        mg, lg, ag = [], [], []
        for p in range(HEADS // 2):
            sl = slice(g * HW + p * 2 * HEAD_DIM, g * HW + (p + 1) * 2 * HEAD_DIM)
            rows = slice(p * 2 * HEAD_DIM, (p + 1) * 2 * HEAD_DIM)
            vtp = c_ref[0, 0, 1, rows, :].astype(BF)
            e_c, e_n = exps[i]
            l2 = jnp.sum(e_c, axis=-1, keepdims=True) + jnp.sum(e_n, axis=-1, keepdims=True)
            acc2 = _dot_nt(e_c.astype(BF), vtp) + _dot(e_n.astype(BF), vb[:, sl])
            mg.append(jnp.where(even, maxes[i][:T], maxes[i][T:]))
            lg.append(jnp.where(even, l2[:T], l2[T:]))
            ag.append(jnp.where(even, acc2[:T], acc2[T:]))
            i += 1
        ms.append(jnp.concatenate(mg, axis=1))
        ls.append(jnp.concatenate(lg, axis=1))
        accs.append(jnp.concatenate(ag, axis=1))
    mx = jnp.maximum(jnp.maximum(ms[0], ms[1]), ms[2])
    ws = [jnp.exp(m - mx) for m in ms]
    num = ws[0] * accs[0] + ws[1] * accs[1] + ws[2] * accs[2]
    den = ws[0] * ls[0] + ws[1] * ls[1] + ws[2] * ls[2]
    yb_ref[...] = (num / den) * _silu(p_ref[:, o_zb:o_zb + HW])


def _sample_mix(l, GW, D, T, proj, cos_s, sin_s, lng, lnb, coef, bias_e, caches):
    n_tok, N = proj.shape
    DB = n_tok // T
    full = lambda *shape: pl.BlockSpec(shape, lambda b: (0,) * len(shape))
    row = lambda w: pl.BlockSpec((T, w), lambda b: (b, 0))
    in_specs = [row(N), full(T, N_DIL * HW), full(T, N_DIL * HW), full(1, GW), full(1, GW), full(T, T, GW),
                full(T, GW)]
    for c in caches:
        in_specs.append(pl.BlockSpec((1, 1, 2, HW, c.shape[-1]), lambda b: (l, b, 0, 0, 0)))
    widths = (GW, HW, D, D, 2 * HW, 2 * HW, 2 * HW, GW)
    return pl.pallas_call(
        functools.partial(_sample_mix_kernel, GW, D),
        grid=(DB,),
        in_specs=in_specs,
        out_specs=[row(w) for w in widths],
        out_shape=[jax.ShapeDtypeStruct((n_tok, w), F32) for w in widths],
        compiler_params=_cp(("arbitrary",)),
        name="sample_mix",
    )(proj, cos_s, sin_s, lng, lnb, coef, bias_e, *caches)


def _sample_out_kernel(last, x_ref, mod_ref, ya_ref, yb_ref, sga_ref, sgb_ref, wgo_ref, wao_ref, wo_ref, fg_ref, o_ref):
    D = x_ref.shape[-1]
    merged = (sga_ref[...] * _dot(ya_ref[...].astype(BF), wgo_ref[...])
              + sgb_ref[...] * _dot(yb_ref[...].astype(BF), wao_ref[...]))
    xo = x_ref[...] + mod_ref[:, 2 * D:3 * D] * _dot(merged.astype(BF), wo_ref[...])
    if last:
        xo = _rms(xo, fg_ref[...])
    o_ref[...] = xo


def _sample_out(last, x, mod, ya, yb, sga, sgb, wgo, wao, wo, fg):
    args = (x, mod, ya, yb, sga, sgb, wgo, wao, wo, fg)
    return pl.pallas_call(
        functools.partial(_sample_out_kernel, last),
        grid=(1,),
        in_specs=[pl.BlockSpec(a.shape, lambda i, nd=a.ndim: (0,) * nd) for a in args],
        out_specs=pl.BlockSpec(x.shape, lambda i: (0, 0)),
        out_shape=jax.ShapeDtypeStruct(x.shape, F32),
        compiler_params=_cp(("arbitrary",)),
        name="sample_out",
    )(*args)


def kernel(x_prompt, x_sample, cache_kv_w128, cache_kv_w512, cache_kv_w2048, c_prompt, c_sample, w_ada, b_ada, norm_g, w_in, gm_ln_g, gm_ln_b, gm_ws, gm_bs, w_gm_out, w_att_out, w_o, final_g):
    B, S, D = x_prompt.shape
    L = w_in.shape[0]
    W3 = N_DIL * HW
    GW = gm_ln_g.shape[-1]
    assert GW == D and S % TS == 0

    c_all = jnp.concatenate([c_prompt, c_sample], axis=0)
    mod_all = _ada(c_all, w_ada.astype(BF), b_ada)

    o_u, o_q, o_zb, o_ga = 0, 3 * GW, 3 * GW + 3 * W3, 3 * GW + 3 * W3 + HW
    o_gb = o_ga + D
    w_in_bf = w_in.astype(BF)
    half = HEAD_DIM // 2
    inv_freq = ROPE_THETA ** (-jnp.arange(half, dtype=F32) / half)
    ang_p = jnp.arange(S, dtype=jnp.int32).astype(F32)[:, None] * inv_freq[None, :]
    cos_p, sin_p = jnp.cos(ang_p).T, jnp.sin(ang_p).T

    DB, T, _ = x_sample.shape
    ang_s = (PAST_LEN + jnp.arange(T, dtype=jnp.int32)).astype(F32)[:, None] * inv_freq[None, :]
    cos_s = jnp.tile(jnp.cos(ang_s), (1, 2 * N_DIL * HEADS))
    sin_s = jnp.tile(jnp.concatenate([-jnp.sin(ang_s), jnp.sin(ang_s)], axis=1), (1, N_DIL * HEADS))
    caches_t = []
    for c in (cache_kv_w128, cache_kv_w512, cache_kv_w2048):
        ct = jnp.transpose(c, (0, 1, 3, 4, 5, 2))
        caches_t.append(ct.reshape(L, DB, 2, HW, c.shape[2]))
    xs = x_sample.reshape(DB * T, D)
    kv_s = [[] for _ in DIL_CONFIGS]
    gm_v = []

    keeps = tuple(min(win, S) for win, _ in DIL_CONFIGS)
    xp = x_prompt
    kv_p = None
    for l in range(L):
        mod_p = mod_all[l, :B].reshape(B, 1, 3 * D)
        ng = norm_g[l].reshape(1, D)
        mod_s = jnp.repeat(mod_all[l, B:], T, axis=0)
        proj_s = _sample_proj(l, xs, mod_s, ng, w_in_bf)
        coef = jnp.repeat(jnp.transpose(gm_ws[l][:, :T, :T], (2, 1, 0)), GW // GM_GROUPS, axis=2)
        bias_s = jnp.repeat(gm_bs[l].T[:T], GW // GM_GROUPS, axis=1)
        so = _sample_mix(l, GW, D, T, proj_s, cos_s, sin_s, gm_ln_g[l].reshape(1, GW), gm_ln_b[l].reshape(1, GW),
                         coef, bias_s, caches_t)
        ya_s, yb_s, sga_s, sgb_s = so[0:4]
        xs = _sample_out(l == L - 1, xs, mod_s, ya_s, yb_s, sga_s, sgb_s, w_gm_out[l].astype(BF),
                         w_att_out[l].astype(BF), w_o[l].astype(BF), final_g.reshape(1, D))
        for g in range(N_DIL):
            kv_s[g].append(so[4 + g].reshape(DB, T, 2, HEADS, HEAD_DIM))
        gm_v.append(so[7].reshape(DB, T, GW))
        wa = jnp.concatenate([w_in_bf[l, :, o_u:o_q], w_in_bf[l, :, o_ga:o_gb]], axis=1)
        wqkv_t = w_in_bf[l, :, o_q:o_zb].T
        wzg = jnp.concatenate([w_in_bf[l, :, o_zb:o_ga], w_in_bf[l, :, o_gb:o_gb + D]], axis=1)
        bias_e = jnp.repeat(gm_bs[l].T, GW // GM_GROUPS, axis=1)
        outs = _qkv(xp, mod_p, ng, wqkv_t, cos_p, sin_p, keeps, l, L, kv_p, wa, gm_ln_g[l].reshape(1, GW),
                    gm_ln_b[l].reshape(1, GW), gm_ws[l], bias_e, w_gm_out[l].astype(BF))
        qs, ks, vs, kv_p, ya = outs[0:3], outs[3:6], outs[6:9], outs[9:12], outs[12]
        ols = [_band_attn(qs[g], ks[g], vs[g]) for g in range(N_DIL)]
        xp = _final(l == L - 1, xp, mod_p, ng, wzg, ya, ols, w_att_out[l].astype(BF), w_o[l].astype(BF),
                    final_g.reshape(1, D))

    def kv_out(a):
        a = a.reshape(a.shape[0], a.shape[1], 2, HEADS, HEAD_DIM, a.shape[-1])
        return jnp.transpose(a, (0, 1, 5, 2, 3, 4))

    return (xp, xs.reshape(DB, T, D), kv_out(kv_p[0]), kv_out(kv_p[1]), kv_out(kv_p[2]),
            jnp.stack(kv_s[0]), jnp.stack(kv_s[1]), jnp.stack(kv_s[2]), jnp.stack(gm_v))
```

```python
import functools

import numpy as np
import jax
import jax.numpy as jnp
from jax import lax
from jax.experimental import pallas as pl
from jax.experimental.pallas import tpu as pltpu

GM_GROUPS = 8
CHUNK = 128
HEAD_DIM = 64
HEADS = 8
DIL_CONFIGS = ((128, 1), (512, 4), (2048, 16))
N_DIL = 3
HW = HEADS * HEAD_DIM
BAND = 128
PAST_LEN = 16384
ROPE_THETA = 10000.0
RMS_EPS = 1e-6
LN_EPS = 1e-5
NEG = -1e30

LANES = 128
TS = 512
TQ = 512
PERM = 256
ATT_BLOCKS = 16
ATT_GROUP = 2
MERGE_ROWS = 256
VMEM_LIMIT = 56 * 1024 * 1024

BF = jnp.bfloat16
F32 = jnp.float32


def _cp(sem, flags=None):
    return pltpu.CompilerParams(dimension_semantics=sem, vmem_limit_bytes=VMEM_LIMIT, flags=flags)


def _gelu(x):
    return 0.5 * x * (1.0 + jnp.tanh(0.7978845608028654 * (x + 0.044715 * (x * x * x))))


def _sigmoid(x):
    return 1.0 / (1.0 + jnp.exp(-x))


def _silu(x):
    return x * _sigmoid(x)


def _rms(x, g):
    return x * lax.rsqrt(jnp.mean(x * x, axis=-1, keepdims=True) + RMS_EPS) * g


def _hmod(x, g, mod, d):
    shift = mod[:, 0:d]
    scale = mod[:, d:2 * d]
    return _rms(x, g) * (1.0 + scale) + shift


def _dot(a, b):
    return jnp.dot(a, b, preferred_element_type=F32)


def _dot_nt(a, b):
    return lax.dot_general(a, b, (((1,), (1,)), ((), ())), preferred_element_type=F32)


def _ada_kernel(c_ref, w_ref, b_ref, o_ref):
    c = c_ref[...]
    o_ref[0] = _dot(_silu(c).astype(BF), w_ref[0].astype(BF)) + b_ref[0]


def _ada(c_all, w_ada, b_ada):
    L, D, D3 = w_ada.shape
    n = c_all.shape[0]
    nt = D3 // D
    return pl.pallas_call(
        _ada_kernel,
        grid=(L, nt),
        in_specs=[
            pl.BlockSpec((n, D), lambda l, j: (0, 0)),
            pl.BlockSpec((1, D, D), lambda l, j: (l, 0, j)),
            pl.BlockSpec((1, 1, D), lambda l, j: (l, 0, j)),
        ],
        out_specs=pl.BlockSpec((1, n, D), lambda l, j: (l, 0, j)),
        out_shape=jax.ShapeDtypeStruct((L, n, D3), F32),
        compiler_params=_cp(("arbitrary", "arbitrary")),
        name="ada_mod",
    )(c_all, w_ada, b_ada.reshape(L, 1, D3))


def _branch_a_norm_v(h, wa_ref, lng_ref, lnb_ref):
    D = lng_ref.shape[-1]
    gv = _gelu(_dot(h, wa_ref[:, D:2 * D]))
    mu = jnp.mean(gv, axis=-1, keepdims=True)
    dv = gv - mu
    var = jnp.mean(dv * dv, axis=-1, keepdims=True)
    return (dv * lax.rsqrt(var + LN_EPS) * lng_ref[...] + lnb_ref[...]).astype(BF)


def _branch_a_gates(h, wa_ref):
    D = wa_ref.shape[0]
    return _gelu(_dot(h, wa_ref[:, 0:D])) * _silu(_dot(h, wa_ref[:, 2 * D:3 * D]))


def _branch_a_mix(h, vn, uz, wa_ref, wm_ref, bias_ref, wgo_ref):
    ts = h.shape[0]
    D = wgo_ref.shape[0]
    row = lax.broadcasted_iota(jnp.int32, (CHUNK, CHUNK), 0)
    col = lax.broadcasted_iota(jnp.int32, (CHUNK, CHUNK), 1)
    causal = row >= col
    nch = ts // CHUNK
    cols = []
    for g in range(GM_GROUPS):
        wmg = jnp.where(causal, wm_ref[g], 0.0).astype(BF)
        vg = jnp.concatenate([vn[c * CHUNK:(c + 1) * CHUNK, g * CHUNK:(g + 1) * CHUNK] for c in range(nch)], axis=1)
        mixed = _dot(wmg, vg)
        cols.append(jnp.concatenate([mixed[:, c * CHUNK:(c + 1) * CHUNK] for c in range(nch)], axis=0))
    sp = jnp.concatenate(cols, axis=1)
    bias = jnp.concatenate([bias_ref[...]] * (ts // CHUNK), axis=0)
    ya = (uz * (sp + bias)).astype(BF)
    ga = _dot(h, wa_ref[:, 3 * D:4 * D])
    return (_sigmoid(ga) * _dot(ya, wgo_ref[...])).astype(BF)


def _rope_t(xt, cos, sin):
    half = HEAD_DIM // 2
    pieces = []
    for h in range(xt.shape[0] // HEAD_DIM):
        x1 = xt[h * HEAD_DIM:h * HEAD_DIM + half]
        x2 = xt[h * HEAD_DIM + half:(h + 1) * HEAD_DIM]
        pieces.append(x1 * cos - x2 * sin)
        pieces.append(x2 * cos + x1 * sin)
    return jnp.concatenate(pieces, axis=0)


def _qkv_kernel(n_alias, x_ref, mod_ref, ng_ref, wt_ref, cos_ref, sin_ref, p1_ref, p4_ref, p16_ref,
                wa_ref, lng_ref, lnb_ref, wm_ref, bias_ref, wgo_ref, *refs):
    (q0_ref, q1_ref, q2_ref, k0_ref, k1_ref, k2_ref, v0_ref, v1_ref, v2_ref,
     kv0_ref, kv1_ref, kv2_ref, ya_ref) = refs[n_alias:]
    D = x_ref.shape[-1]
    h = _hmod(x_ref[0], ng_ref[...], mod_ref[0], D).astype(BF)
    w = N_DIL * HW
    cos = cos_ref[...]
    sin = sin_ref[...]
    p_refs = (p1_ref, p4_ref, p16_ref)
    dst_refs = ((q0_ref, q1_ref, q2_ref), (k0_ref, k1_ref, k2_ref), (v0_ref, v1_ref, v2_ref))
    kv_refs = (kv0_ref, kv1_ref, kv2_ref)
    chunks = [(kind, g) for g in range(N_DIL) for kind in range(3)]

    def project(kind, g):
        r0 = kind * w + g * HW
        return _dot_nt(wt_ref[r0:r0 + HW, :], h)

    vn = _branch_a_norm_v(h, wa_ref, lng_ref, lnb_ref)
    uz = _branch_a_gates(h, wa_ref)
    raw = project(*chunks[0])
    for i, (kind, g) in enumerate(chunks):
        nxt = project(*chunks[i + 1]) if i + 1 < len(chunks) else None
        if kind == 0:
            xt = _rope_t(raw, cos, sin) * (HEAD_DIM ** -0.5)
        elif kind == 1:
            xt = _rope_t(raw, cos, sin)
        else:
            xt = raw
        if kind > 0:
            kv_ref = kv_refs[g]
            kv_ref[0, 0, kind - 1] = xt[:, TQ - kv_ref.shape[-1]:]

        dst = dst_refs[kind][g]
        d = dst.shape[1]
        per = PERM // d
        if d == 1:
            dst[0, 0] = jnp.transpose(xt).astype(BF)
        else:
            xb = xt.astype(BF)
            for hf in range(TQ // PERM):
                nat = _dot_nt(p_refs[g][...], xb[:, hf * PERM:(hf + 1) * PERM]).astype(BF)
                dst[0, :, hf * per:(hf + 1) * per, :] = nat.reshape(d, per, HW)
        raw = nxt
    ya_ref[0] = _branch_a_mix(h, vn, uz, wa_ref, wm_ref, bias_ref, wgo_ref)


def _perm_t(d):
    m = np.zeros((PERM, PERM), np.float32)
    per = PERM // d
    for r in range(d):
        for i in range(per):
            m[r * per + i, i * d + r] = 1.0
    return jnp.asarray(m, dtype=BF)


def _qkv(x, mod, ng, wt, cos_t, sin_t, keeps, layer, n_layers, kv_prev, wa, lng, lnb, wm, bias_e, wgo):
    B, S, D = x.shape
    nt = S // TQ
    full = lambda *shape: pl.BlockSpec(shape, lambda b, t: (0,) * len(shape))
    in_specs = [
        pl.BlockSpec((1, TQ, D), lambda b, t: (b, t, 0)),
        pl.BlockSpec((1, 1, 3 * D), lambda b, t: (b, 0, 0)),
        full(1, D), full(3 * N_DIL * HW, D),
        pl.BlockSpec((HEAD_DIM // 2, TQ), lambda b, t: (0, t)),
        pl.BlockSpec((HEAD_DIM // 2, TQ), lambda b, t: (0, t)),
        full(PERM, PERM), full(PERM, PERM), full(PERM, PERM),
        full(D, 4 * D), full(1, D), full(1, D), full(GM_GROUPS, CHUNK, CHUNK), full(CHUNK, D), full(D, D),
    ]
    out_specs, out_shape = [], []
    for _ in range(3):
        for (_, d) in DIL_CONFIGS:
            out_specs.append(pl.BlockSpec((1, d, TQ // d, HW), lambda b, t: (b, 0, t, 0)))
            out_shape.append(jax.ShapeDtypeStruct((B, d, S // d, HW), BF))
    for keep in keeps:
        wcols = min(keep, TQ)
        first = (S - keep) // TQ if keep >= TQ else nt - 1
        out_specs.append(pl.BlockSpec(
            (1, 1, 2, HW, wcols),
            functools.partial(lambda b, t, f: (layer, b, 0, 0, jnp.maximum(t - f, 0)), f=first)))
        out_shape.append(jax.ShapeDtypeStruct((n_layers, B, 2, HW, keep), F32))
    out_specs.append(pl.BlockSpec((1, TQ, D), lambda b, t: (b, t, 0)))
    out_shape.append(jax.ShapeDtypeStruct((B, S, D), BF))
    args = [x, mod, ng, wt, cos_t, sin_t, _perm_t(1), _perm_t(4), _perm_t(16), wa, lng, lnb, wm, bias_e, wgo]
    aliases = {}
    if kv_prev is not None:
        for i, a in enumerate(kv_prev):
            aliases[len(args)] = 3 * N_DIL + i
            in_specs.append(pl.BlockSpec(memory_space=pl.ANY))
            args.append(a)
    n_alias = len(aliases)
    return pl.pallas_call(
        functools.partial(_qkv_kernel, n_alias),
        grid=(B, nt),
        in_specs=in_specs,
        out_specs=out_specs,
        out_shape=out_shape,
        input_output_aliases=aliases,
        compiler_params=_cp(("arbitrary", "arbitrary")),
        name="proj",
    )(*args)


def _band_window(n, qb):
    win = min(2 * BAND, n)
    return max(qb - 1, 0) * BAND if n > win else 0, win


def _band_bias(n, qb):
    start, win = _band_window(n, qb)
    dist = (qb * BAND + np.arange(BAND)[:, None]) - (start + np.arange(win)[None, :])
    return np.where((dist >= 0) & (dist <= BAND), 0.0, NEG).astype(np.float32)


def _band_attn_blocks(n, blocks, q_ref, k_ref, v_ref, b0_ref, b1_ref, o_ref, l_ref):
    lane = lax.broadcasted_iota(jnp.int32, (1, 2 * HEAD_DIM), 1)
    even = lane < HEAD_DIM
    pairs = [slice(p * 2 * HEAD_DIM, (p + 1) * 2 * HEAD_DIM) for p in range(HEADS // 2)]
    ss, vws = [], []
    for rr, qb in blocks:
        start, win = _band_window(n, qb)
        q = q_ref[0, rr, qb * BAND:(qb + 1) * BAND, :]
        kw = k_ref[0, rr, start:start + win, :]
        vws.append(v_ref[0, rr, start:start + win, :])
        bias = b0_ref[...] if qb == 0 else b1_ref[...]
        for sl in pairs:
            qp, kp = q[:, sl], kw[:, sl]
            for sel in (even, ~even):
                ss.append(_dot_nt(jnp.where(sel, qp, jnp.zeros_like(qp)), kp) + bias)
    ms = [jnp.max(s, axis=-1, keepdims=True) for s in ss]
    es = [jnp.exp(s - m).astype(BF) for s, m in zip(ss, ms)]
    for bi, (rr, qb) in enumerate(blocks):
        vw = vws[bi]
        rows = slice(qb * BAND, (qb + 1) * BAND)
        ones = jnp.ones((vw.shape[0], 2 * HEAD_DIM), BF)
        for p, sl in enumerate(pairs):
            i0 = bi * HEADS + 2 * p
            va = jnp.concatenate([vw[:, sl], ones], axis=1)
            r0 = _dot(es[i0], va)
            r1 = _dot(es[i0 + 1], va)
            acc = jnp.where(even, r0[:, :2 * HEAD_DIM], r1[:, :2 * HEAD_DIM])
            den = jnp.where(even, r0[:, 2 * HEAD_DIM:], r1[:, 2 * HEAD_DIM:])
            o_ref[0, rr, rows, sl] = (acc / den).astype(o_ref.dtype)
            l_ref[0, rr, rows, sl] = jnp.where(even, ms[i0], ms[i0 + 1]) + jnp.log(den)


def _band_attn_kernel(n, q_ref, k_ref, v_ref, b0_ref, b1_ref, o_ref, l_ref):
    blocks = [(rr, qb) for rr in range(q_ref.shape[1]) for qb in range(n // BAND)]
    group = ATT_GROUP if n <= BAND else 1
    for i in range(0, len(blocks), group):
        _band_attn_blocks(n, blocks[i:i + group], q_ref, k_ref, v_ref, b0_ref, b1_ref, o_ref, l_ref)


def _band_attn(q, k, v):
    B, d, n, _ = q.shape
    nqb = n // BAND
    nres = max(ATT_BLOCKS // nqb, 1)
    b0 = jnp.asarray(_band_bias(n, 0))
    b1 = jnp.asarray(_band_bias(n, min(1, nqb - 1)))
    blk = pl.BlockSpec((1, nres, n, HW), lambda b, r: (b, r, 0, 0))
    mask = pl.BlockSpec(b0.shape, lambda b, r: (0, 0))
    return pl.pallas_call(
        functools.partial(_band_attn_kernel, n),
        grid=(B, d // nres),
        in_specs=[blk, blk, blk, mask, mask],
        out_specs=[blk, blk],
        out_shape=[jax.ShapeDtypeStruct((B, d, n, HW), BF), jax.ShapeDtypeStruct((B, d, n, HW), F32)],
        compiler_params=_cp(("arbitrary", "arbitrary")),
        name=f"band_attn_d{d}",
    )(q, k, v, b0, b1)


def _final_kernel(last, x_ref, mod_ref, ng_ref, wzg_ref, ya_ref, o0_ref, l0_ref, o1_ref, l1_ref, o2_ref, l2_ref,
                  wao_ref, wo_ref, fg_ref, out_ref, no_ref, nl_ref):
    D = x_ref.shape[-1]
    mod = mod_ref[0]
    gate = mod[:, 2 * D:3 * D]
    nlt = HW // LANES
    nsub = TS // MERGE_ROWS
    subs = [slice(i * MERGE_ROWS, (i + 1) * MERGE_ROWS) for i in range(nsub)]

    def attention_rows(i):
        outs, lses = [], []
        for g, (o_ref, l_ref) in enumerate(((o0_ref, l0_ref), (o1_ref, l1_ref), (o2_ref, l2_ref))):
            d = o_ref.shape[1]
            per = MERGE_ROWS // d
            src = slice(i * per, (i + 1) * per)
            if d == 1:
                outs.append(o_ref[0, 0, src, :].astype(F32))
                lses.append(l_ref[0, 0, src, :])
                continue
            for r in range(d):
                ov = o_ref[0, r, src, :].astype(F32)
                lv = l_ref[0, r, src, :]
                for c in range(nlt):
                    dst = pl.ds(i * MERGE_ROWS + r, per, stride=d)
                    no_ref[g * nlt + c, dst, :] = ov[:, c * LANES:(c + 1) * LANES]
                    nl_ref[g * nlt + c, dst, :] = lv[:, c * LANES:(c + 1) * LANES]
            outs.append(jnp.concatenate([no_ref[g * nlt + c, subs[i], :] for c in range(nlt)], axis=1))
            lses.append(jnp.concatenate([nl_ref[g * nlt + c, subs[i], :] for c in range(nlt)], axis=1))
        mx = jnp.maximum(jnp.maximum(lses[0], lses[1]), lses[2])
        ws = [jnp.exp(l - mx) for l in lses]
        den = ws[0] + ws[1] + ws[2]
        return (ws[0] * outs[0] + ws[1] * outs[1] + ws[2] * outs[2]) / den

    hs = [_hmod(x_ref[0, sl, :], ng_ref[...], mod, D).astype(BF) for sl in subs]
    zgs = [_dot(h, wzg_ref[...]) for h in hs]
    atts = [attention_rows(i) for i in range(nsub)]
    mbs = []
    for i in range(nsub):
        yb = (atts[i] * _silu(zgs[i][:, 0:HW])).astype(BF)
        mbs.append(_dot(yb, wao_ref[...]))
    for i, sl in enumerate(subs):
        merged = ya_ref[0, sl, :].astype(F32) + _sigmoid(zgs[i][:, HW:HW + D]) * mbs[i]
        xo = x_ref[0, sl, :] + gate * _dot(merged.astype(BF), wo_ref[...])
        if last:
            xo = _rms(xo, fg_ref[...])
        out_ref[0, sl, :] = xo


def _final(last, x, mod, ng, wzg, ya, ols, wao, wo, fg):
    B, S, D = x.shape
    nt = S // TS
    full = lambda *shape: pl.BlockSpec(shape, lambda b, t: (0,) * len(shape))
    tile = pl.BlockSpec((1, TS, D), lambda b, t: (b, t, 0))
    in_specs = [tile, pl.BlockSpec((1, 1, 3 * D), lambda b, t: (b, 0, 0)), full(1, D), full(D, HW + D), tile]
    args = [x, mod, ng, wzg, ya]
    for (o, l), (_, d) in zip(ols, DIL_CONFIGS):
        spec = pl.BlockSpec((1, d, TS // d, HW), lambda b, t: (b, 0, t, 0))
        in_specs += [spec, spec]
        args += [o, l]
    in_specs += [full(HW, D), full(D, D), full(1, D)]
    args += [wao, wo, fg]
    return pl.pallas_call(
        functools.partial(_final_kernel, last),
        grid=(B, nt),
        in_specs=in_specs,
        out_specs=tile,
        out_shape=jax.ShapeDtypeStruct((B, S, D), F32),
        scratch_shapes=[pltpu.VMEM((N_DIL * HW // LANES, TS, LANES), F32)] * 2,
        compiler_params=_cp(("arbitrary", "arbitrary")),
        name="merge_out",
    )(*args)


def _sample_proj_kernel(x_ref, mod_ref, ng_ref, w_ref, o_ref):
    D = x_ref.shape[-1]
    h = _hmod(x_ref[...], ng_ref[...], mod_ref[...], D).astype(BF)
    o_ref[...] = _dot(h, w_ref[0])


def _sample_proj(l, x, mod, ng, w_in_bf):
    n, D = x.shape
    N = w_in_bf.shape[-1]
    return pl.pallas_call(
        _sample_proj_kernel,
        grid=(N // D,),
        in_specs=[
            pl.BlockSpec((n, D), lambda j: (0, 0)),
            pl.BlockSpec((n, 3 * D), lambda j: (0, 0)),
            pl.BlockSpec((1, D), lambda j: (0, 0)),
            pl.BlockSpec((1, D, D), lambda j: (l, 0, j)),
        ],
        out_specs=pl.BlockSpec((n, D), lambda j: (0, j)),
        out_shape=jax.ShapeDtypeStruct((n, N), F32),
        compiler_params=_cp(("arbitrary",)),
        name="sample_proj",
    )(x, mod, ng, w_in_bf)


def _rope_nat(x, cos_full, sin_signed):
    lane = lax.broadcasted_iota(jnp.int32, (1, LANES), 1)
    first = (lane & (HEAD_DIM - 1)) < HEAD_DIM // 2
    chunks = []
    for c in range(x.shape[1] // LANES):
        sl = slice(c * LANES, (c + 1) * LANES)
        xc = x[:, sl]
        rot = jnp.where(first, pltpu.roll(xc, LANES - HEAD_DIM // 2, 1), pltpu.roll(xc, HEAD_DIM // 2, 1))
        chunks.append(xc * cos_full[:, sl] + rot * sin_signed[:, sl])
    return jnp.concatenate(chunks, axis=1)


def _sample_mix_kernel(GW, D, p_ref, cos_ref, sin_ref, lng_ref, lnb_ref, coef_ref, bias_ref, c0_ref, c1_ref, c2_ref,
                       ya_ref, yb_ref, sga_ref, sgb_ref, kv0_ref, kv1_ref, kv2_ref, vn_ref):
    T = p_ref.shape[0]
    W3 = N_DIL * HW
    o_q = 3 * GW
    o_zb = o_q + 3 * W3
    o_ga = o_zb + HW
    o_gb = o_ga + D
    u = p_ref[:, 0:GW]
    v = p_ref[:, GW:2 * GW]
    za = p_ref[:, 2 * GW:3 * GW]
    gv = _gelu(v)
    mu = jnp.mean(gv, axis=-1, keepdims=True)
    dv = gv - mu
    var = jnp.mean(dv * dv, axis=-1, keepdims=True)
    vn = dv * lax.rsqrt(var + LN_EPS) * lng_ref[...] + lnb_ref[...]
    vn_ref[...] = vn
    vnb = vn.astype(BF).astype(F32)
    trow = lax.broadcasted_iota(jnp.int32, (T, GW), 0)
    sp = None
    for s in range(T):
        cf = jnp.where(trow >= s, coef_ref[s], 0.0).astype(BF).astype(F32)
        term = cf * vnb[s:s + 1, :]
        sp = term if sp is None else sp + term
    ya_ref[...] = _gelu(u) * (sp + bias_ref[...]) * _silu(za)
    sga_ref[...] = _sigmoid(p_ref[:, o_ga:o_ga + D])
    sgb_ref[...] = _sigmoid(p_ref[:, o_gb:o_gb + D])
    q = _rope_nat(p_ref[:, o_q:o_q + W3], cos_ref[...], sin_ref[...]) * (HEAD_DIM ** -0.5)
    k = _rope_nat(p_ref[:, o_q + W3:o_q + 2 * W3], cos_ref[...], sin_ref[...])
    val = p_ref[:, o_q + 2 * W3:o_q + 3 * W3]
    qb, kb, vb = q.astype(BF), k.astype(BF), val.astype(BF)
    lane = lax.broadcasted_iota(jnp.int32, (1, 2 * HEAD_DIM), 1)
    even = lane < HEAD_DIM
    groups = list(zip(DIL_CONFIGS, (c0_ref, c1_ref, c2_ref), (kv0_ref, kv1_ref, kv2_ref)))
    row_half = jnp.where(lax.broadcasted_iota(jnp.int32, (2 * T, 2 * HEAD_DIM), 0) >= T, 1, 0)
    lane_half = jnp.where(lax.broadcasted_iota(jnp.int32, (2 * T, 2 * HEAD_DIM), 1) >= HEAD_DIM, 1, 0)
    keep2 = row_half == lane_half
    scores = []
    for g, ((win, d), c_ref, kv_ref) in enumerate(groups):
        n = c_ref.shape[-1]
        kv_ref[:, 0:HW] = k[:, g * HW:(g + 1) * HW]
        kv_ref[:, HW:2 * HW] = val[:, g * HW:(g + 1) * HW]
        tq = lax.broadcasted_iota(jnp.int32, (2 * T, n), 0) & (T - 1)
        dist_c = n + tq - lax.broadcasted_iota(jnp.int32, (2 * T, n), 1)
        bias_c = jnp.where((dist_c <= win) & ((dist_c & (d - 1)) == 0), 0.0, NEG)
        dist_n = ((lax.broadcasted_iota(jnp.int32, (2 * T, T), 0) & (T - 1))
                  - lax.broadcasted_iota(jnp.int32, (2 * T, T), 1))
        bias_n = jnp.where((dist_n >= 0) & (dist_n <= win) & ((dist_n & (d - 1)) == 0), 0.0, NEG)
        for p in range(HEADS // 2):
            sl = slice(g * HW + p * 2 * HEAD_DIM, g * HW + (p + 1) * 2 * HEAD_DIM)
            rows = slice(p * 2 * HEAD_DIM, (p + 1) * 2 * HEAD_DIM)
            ktp = c_ref[0, 0, 0, rows, :].astype(BF)
            qp = qb[:, sl]
            q2 = jnp.concatenate([qp, qp], axis=0)
            q2 = jnp.where(keep2, q2, jnp.zeros_like(q2))
            scores.append((_dot(q2, ktp) + bias_c, _dot_nt(q2, kb[:, sl]) + bias_n))
    maxes = [jnp.maximum(jnp.max(s_c, axis=-1, keepdims=True), jnp.max(s_n, axis=-1, keepdims=True))
             for s_c, s_n in scores]
    exps = [(jnp.exp(s_c - m), jnp.exp(s_n - m)) for (s_c, s_n), m in zip(scores, maxes)]
    ms, ls, accs = [], [], []
    i = 0
    for g, ((win, d), c_ref, kv_ref) in enumerate(groups):
        mg, lg, ag = [], [], []
        for p in range(HEADS // 2):
            sl = slice(g * HW + p * 2 * HEAD_DIM, g * HW + (p + 1) * 2 * HEAD_DIM)
            rows = slice(p * 2 * HEAD_DIM, (p + 1) * 2 * HEAD_DIM)
            vtp = c_ref[0, 0, 1, rows, :].astype(BF)
            e_c, e_n = exps[i]
            l2 = jnp.sum(e_c, axis=-1, keepdims=True) + jnp.sum(e_n, axis=-1, keepdims=True)
            acc2 = _dot_nt(e_c.astype(BF), vtp) + _dot(e_n.astype(BF), vb[:, sl])
            mg.append(jnp.where(even, maxes[i][:T], maxes[i][T:]))
            lg.append(jnp.where(even, l2[:T], l2[T:]))
            ag.append(jnp.where(even, acc2[:T], acc2[T:]))
            i += 1
        ms.append(jnp.concatenate(mg, axis=1))
        ls.append(jnp.concatenate(lg, axis=1))
        accs.append(jnp.concatenate(ag, axis=1))
    mx = jnp.maximum(jnp.maximum(ms[0], ms[1]), ms[2])
    ws = [jnp.exp(m - mx) for m in ms]
    num = ws[0] * accs[0] + ws[1] * accs[1] + ws[2] * accs[2]
    den = ws[0] * ls[0] + ws[1] * ls[1] + ws[2] * ls[2]
    yb_ref[...] = (num / den) * _silu(p_ref[:, o_zb:o_zb + HW])


def _sample_mix(l, GW, D, T, proj, cos_s, sin_s, lng, lnb, coef, bias_e, caches):
    n_tok, N = proj.shape
    DB = n_tok // T
    full = lambda *shape: pl.BlockSpec(shape, lambda b: (0,) * len(shape))
    row = lambda w: pl.BlockSpec((T, w), lambda b: (b, 0))
    in_specs = [row(N), full(T, N_DIL * HW), full(T, N_DIL * HW), full(1, GW), full(1, GW), full(T, T, GW),
                full(T, GW)]
    for c in caches:
        in_specs.append(pl.BlockSpec((1, 1, 2, HW, c.shape[-1]), lambda b: (l, b, 0, 0, 0)))
    widths = (GW, HW, D, D, 2 * HW, 2 * HW, 2 * HW, GW)
    return pl.pallas_call(
        functools.partial(_sample_mix_kernel, GW, D),
        grid=(DB,),
        in_specs=in_specs,
        out_specs=[row(w) for w in widths],
        out_shape=[jax.ShapeDtypeStruct((n_tok, w), F32) for w in widths],
        compiler_params=_cp(("arbitrary",)),
        name="sample_mix",
    )(proj, cos_s, sin_s, lng, lnb, coef, bias_e, *caches)


def _sample_out_kernel(last, x_ref, mod_ref, ya_ref, yb_ref, sga_ref, sgb_ref, wgo_ref, wao_ref, wo_ref, fg_ref, o_ref):
    D = x_ref.shape[-1]
    merged = (sga_ref[...] * _dot(ya_ref[...].astype(BF), wgo_ref[...])
              + sgb_ref[...] * _dot(yb_ref[...].astype(BF), wao_ref[...]))
    xo = x_ref[...] + mod_ref[:, 2 * D:3 * D] * _dot(merged.astype(BF), wo_ref[...])
    if last:
        xo = _rms(xo, fg_ref[...])
    o_ref[...] = xo


def _sample_out(last, x, mod, ya, yb, sga, sgb, wgo, wao, wo, fg):
    args = (x, mod, ya, yb, sga, sgb, wgo, wao, wo, fg)
    return pl.pallas_call(
        functools.partial(_sample_out_kernel, last),
        grid=(1,),
        in_specs=[pl.BlockSpec(a.shape, lambda i, nd=a.ndim: (0,) * nd) for a in args],
        out_specs=pl.BlockSpec(x.shape, lambda i: (0, 0)),
        out_shape=jax.ShapeDtypeStruct(x.shape, F32),
        compiler_params=_cp(("arbitrary",)),
        name="sample_out",
    )(*args)


def kernel(x_prompt, x_sample, cache_kv_w128, cache_kv_w512, cache_kv_w2048, c_prompt, c_sample, w_ada, b_ada, norm_g, w_in, gm_ln_g, gm_ln_b, gm_ws, gm_bs, w_gm_out, w_att_out, w_o, final_g):
    B, S, D = x_prompt.shape
    L = w_in.shape[0]
    W3 = N_DIL * HW
    GW = gm_ln_g.shape[-1]
    assert GW == D and S % TS == 0

    c_all = jnp.concatenate([c_prompt, c_sample], axis=0)
    mod_all = _ada(c_all, w_ada, b_ada)

    o_u, o_q, o_zb, o_ga = 0, 3 * GW, 3 * GW + 3 * W3, 3 * GW + 3 * W3 + HW
    o_gb = o_ga + D
    w_in_bf = w_in.astype(BF)
    half = HEAD_DIM // 2
    inv_freq = ROPE_THETA ** (-jnp.arange(half, dtype=F32) / half)
    ang_p = jnp.arange(S, dtype=jnp.int32).astype(F32)[:, None] * inv_freq[None, :]
    cos_p, sin_p = jnp.cos(ang_p).T, jnp.sin(ang_p).T

    DB, T, _ = x_sample.shape
    ang_s = (PAST_LEN + jnp.arange(T, dtype=jnp.int32)).astype(F32)[:, None] * inv_freq[None, :]
    cos_s = jnp.tile(jnp.cos(ang_s), (1, 2 * N_DIL * HEADS))
    sin_s = jnp.tile(jnp.concatenate([-jnp.sin(ang_s), jnp.sin(ang_s)], axis=1), (1, N_DIL * HEADS))
    caches_t = []
    for c in (cache_kv_w128, cache_kv_w512, cache_kv_w2048):
        ct = jnp.transpose(c, (0, 1, 3, 4, 5, 2))
        caches_t.append(ct.reshape(L, DB, 2, HW, c.shape[2]))
    xs = x_sample.reshape(DB * T, D)
    kv_s = [[] for _ in DIL_CONFIGS]
    gm_v = []

    keeps = tuple(min(win, S) for win, _ in DIL_CONFIGS)
    xp = x_prompt
    kv_p = None
    for l in range(L):
        mod_p = mod_all[l, :B].reshape(B, 1, 3 * D)
        ng = norm_g[l].reshape(1, D)
        mod_s = jnp.repeat(mod_all[l, B:], T, axis=0)
        proj_s = _sample_proj(l, xs, mod_s, ng, w_in_bf)
        coef = jnp.repeat(jnp.transpose(gm_ws[l][:, :T, :T], (2, 1, 0)), GW // GM_GROUPS, axis=2)
        bias_s = jnp.repeat(gm_bs[l].T[:T], GW // GM_GROUPS, axis=1)
        so = _sample_mix(l, GW, D, T, proj_s, cos_s, sin_s, gm_ln_g[l].reshape(1, GW), gm_ln_b[l].reshape(1, GW),
                         coef, bias_s, caches_t)
        ya_s, yb_s, sga_s, sgb_s = so[0:4]
        xs = _sample_out(l == L - 1, xs, mod_s, ya_s, yb_s, sga_s, sgb_s, w_gm_out[l].astype(BF),
                         w_att_out[l].astype(BF), w_o[l].astype(BF), final_g.reshape(1, D))
        for g in range(N_DIL):
            kv_s[g].append(so[4 + g].reshape(DB, T, 2, HEADS, HEAD_DIM))
        gm_v.append(so[7].reshape(DB, T, GW))
        wa = jnp.concatenate([w_in_bf[l, :, o_u:o_q], w_in_bf[l, :, o_ga:o_gb]], axis=1)
        wqkv_t = w_in_bf[l, :, o_q:o_zb].T
        wzg = jnp.concatenate([w_in_bf[l, :, o_zb:o_ga], w_in_bf[l, :, o_gb:o_gb + D]], axis=1)
        bias_e = jnp.repeat(gm_bs[l].T, GW // GM_GROUPS, axis=1)
        outs = _qkv(xp, mod_p, ng, wqkv_t, cos_p, sin_p, keeps, l, L, kv_p, wa, gm_ln_g[l].reshape(1, GW),
                    gm_ln_b[l].reshape(1, GW), gm_ws[l], bias_e, w_gm_out[l].astype(BF))
        qs, ks, vs, kv_p, ya = outs[0:3], outs[3:6], outs[6:9], outs[9:12], outs[12]
        ols = [_band_attn(qs[g], ks[g], vs[g]) for g in range(N_DIL)]
        xp = _final(l == L - 1, xp, mod_p, ng, wzg, ya, ols, w_att_out[l].astype(BF), w_o[l].astype(BF),
                    final_g.reshape(1, D))

    def kv_out(a):
        a = a.reshape(a.shape[0], a.shape[1], 2, HEADS, HEAD_DIM, a.shape[-1])
        return jnp.transpose(a, (0, 1, 5, 2, 3, 4))

    return (xp, xs.reshape(DB, T, D), kv_out(kv_p[0]), kv_out(kv_p[1]), kv_out(kv_p[2]),
            jnp.stack(kv_s[0]), jnp.stack(kv_s[1]), jnp.stack(kv_s[2]), jnp.stack(gm_v))
```

```python
import functools

import numpy as np
import jax
import jax.numpy as jnp
from jax import lax
from jax.experimental import pallas as pl
from jax.experimental.pallas import tpu as pltpu

GM_GROUPS = 8
CHUNK = 128
HEAD_DIM = 64
HEADS = 8
DIL_CONFIGS = ((128, 1), (512, 4), (2048, 16))
N_DIL = 3
HW = HEADS * HEAD_DIM
BAND = 128
PAST_LEN = 16384
ROPE_THETA = 10000.0
RMS_EPS = 1e-6
LN_EPS = 1e-5
NEG = -1e30

LANES = 128
TS = 512
TQ = 512
PERM = 256
ATT_BLOCKS = 16
ATT_GROUP = 2
MERGE_ROWS = 256
VMEM_LIMIT = 56 * 1024 * 1024

BF = jnp.bfloat16
F32 = jnp.float32


def _cp(sem, flags=None):
    return pltpu.CompilerParams(dimension_semantics=sem, vmem_limit_bytes=VMEM_LIMIT, flags=flags)


def _gelu(x):
    return 0.5 * x * (1.0 + jnp.tanh(0.7978845608028654 * (x + 0.044715 * (x * x * x))))


def _sigmoid(x):
    return 1.0 / (1.0 + jnp.exp(-x))


def _silu(x):
    return x * _sigmoid(x)


def _rms(x, g):
    return x * lax.rsqrt(jnp.mean(x * x, axis=-1, keepdims=True) + RMS_EPS) * g


def _hmod(x, g, mod, d):
    shift = mod[:, 0:d]
    scale = mod[:, d:2 * d]
    return _rms(x, g) * (1.0 + scale) + shift


def _dot(a, b):
    return jnp.dot(a, b, preferred_element_type=F32)


def _dot_nt(a, b):
    return lax.dot_general(a, b, (((1,), (1,)), ((), ())), preferred_element_type=F32)


def _ada_kernel(c_ref, w_ref, b_ref, o_ref):
    c = c_ref[...]
    o_ref[0] = _dot(_silu(c).astype(BF), w_ref[0].astype(BF)) + b_ref[0]


def _ada(c_all, w_ada, b_ada):
    L, D, D3 = w_ada.shape
    n = c_all.shape[0]
    nt = D3 // D
    return pl.pallas_call(
        _ada_kernel,
        grid=(L, nt),
        in_specs=[
            pl.BlockSpec((n, D), lambda l, j: (0, 0)),
            pl.BlockSpec((1, D, D), lambda l, j: (l, 0, j)),
            pl.BlockSpec((1, 1, D), lambda l, j: (l, 0, j)),
        ],
        out_specs=pl.BlockSpec((1, n, D), lambda l, j: (l, 0, j)),
        out_shape=jax.ShapeDtypeStruct((L, n, D3), F32),
        compiler_params=_cp(("arbitrary", "arbitrary")),
        name="ada_mod",
    )(c_all, w_ada, b_ada.reshape(L, 1, D3))


def _branch_a_norm_v(h, wa_ref, lng_ref, lnb_ref):
    D = lng_ref.shape[-1]
    gv = _gelu(_dot(h, wa_ref[:, D:2 * D]))
    mu = jnp.mean(gv, axis=-1, keepdims=True)
    dv = gv - mu
    var = jnp.mean(dv * dv, axis=-1, keepdims=True)
    return (dv * lax.rsqrt(var + LN_EPS) * lng_ref[...] + lnb_ref[...]).astype(BF)


def _branch_a_gates(h, wa_ref):
    D = wa_ref.shape[0]
    return _gelu(_dot(h, wa_ref[:, 0:D])) * _silu(_dot(h, wa_ref[:, 2 * D:3 * D]))


def _branch_a_mix(h, vn, uz, wa_ref, wm_ref, bias_ref, wgo_ref):
    ts = h.shape[0]
    D = wgo_ref.shape[0]
    row = lax.broadcasted_iota(jnp.int32, (CHUNK, CHUNK), 0)
    col = lax.broadcasted_iota(jnp.int32, (CHUNK, CHUNK), 1)
    causal = row >= col
    nch = ts // CHUNK
    cols = []
    for g in range(GM_GROUPS):
        wmg = jnp.where(causal, wm_ref[g], 0.0).astype(BF)
        vg = jnp.concatenate([vn[c * CHUNK:(c + 1) * CHUNK, g * CHUNK:(g + 1) * CHUNK] for c in range(nch)], axis=1)
        mixed = _dot(wmg, vg)
        cols.append(jnp.concatenate([mixed[:, c * CHUNK:(c + 1) * CHUNK] for c in range(nch)], axis=0))
    sp = jnp.concatenate(cols, axis=1)
    bias = jnp.concatenate([bias_ref[...]] * (ts // CHUNK), axis=0)
    ya = (uz * (sp + bias)).astype(BF)
    ga = _dot(h, wa_ref[:, 3 * D:4 * D])
    return (_sigmoid(ga) * _dot(ya, wgo_ref[...])).astype(BF)


def _rope_t(xt, cos, sin):
    half = HEAD_DIM // 2
    pieces = []
    for h in range(xt.shape[0] // HEAD_DIM):
        x1 = xt[h * HEAD_DIM:h * HEAD_DIM + half]
        x2 = xt[h * HEAD_DIM + half:(h + 1) * HEAD_DIM]
        pieces.append(x1 * cos - x2 * sin)
        pieces.append(x2 * cos + x1 * sin)
    return jnp.concatenate(pieces, axis=0)


def _qkv_kernel(n_alias, x_ref, mod_ref, ng_ref, wt_ref, cos_ref, sin_ref, p1_ref, p4_ref, p16_ref,
                wa_ref, lng_ref, lnb_ref, wm_ref, bias_ref, wgo_ref, *refs):
    (q0_ref, q1_ref, q2_ref, k0_ref, k1_ref, k2_ref, v0_ref, v1_ref, v2_ref,
     kv0_ref, kv1_ref, kv2_ref, ya_ref) = refs[n_alias:]
    D = x_ref.shape[-1]
    wt_ref, wa_ref, wgo_ref = wt_ref.at[0], wa_ref.at[0], wgo_ref.at[0]
    h = _hmod(x_ref[0], ng_ref[...], mod_ref[0], D).astype(BF)
    w = N_DIL * HW
    cos = cos_ref[...]
    sin = sin_ref[...]
    p_refs = (p1_ref, p4_ref, p16_ref)
    dst_refs = ((q0_ref, q1_ref, q2_ref), (k0_ref, k1_ref, k2_ref), (v0_ref, v1_ref, v2_ref))
    kv_refs = (kv0_ref, kv1_ref, kv2_ref)
    chunks = [(kind, g) for g in range(N_DIL) for kind in range(3)]

    def project(kind, g):
        r0 = kind * w + g * HW
        return _dot_nt(wt_ref[r0:r0 + HW, :], h)

    vn = _branch_a_norm_v(h, wa_ref, lng_ref, lnb_ref)
    uz = _branch_a_gates(h, wa_ref)
    raw = project(*chunks[0])
    for i, (kind, g) in enumerate(chunks):
        nxt = project(*chunks[i + 1]) if i + 1 < len(chunks) else None
        if kind == 0:
            xt = _rope_t(raw, cos, sin) * (HEAD_DIM ** -0.5)
        elif kind == 1:
            xt = _rope_t(raw, cos, sin)
        else:
            xt = raw
        if kind > 0:
            kv_ref = kv_refs[g]
            kv_ref[0, 0, kind - 1] = xt[:, TQ - kv_ref.shape[-1]:]

        dst = dst_refs[kind][g]
        d = dst.shape[1]
        per = PERM // d
        if d == 1:
            dst[0, 0] = jnp.transpose(xt).astype(BF)
        else:
            xb = xt.astype(BF)
            for hf in range(TQ // PERM):
                nat = _dot_nt(p_refs[g][...], xb[:, hf * PERM:(hf + 1) * PERM]).astype(BF)
                dst[0, :, hf * per:(hf + 1) * per, :] = nat.reshape(d, per, HW)
        raw = nxt
    ya_ref[0] = _branch_a_mix(h, vn, uz, wa_ref, wm_ref, bias_ref, wgo_ref)


def _perm_t(d):
    m = np.zeros((PERM, PERM), np.float32)
    per = PERM // d
    for r in range(d):
        for i in range(per):
            m[r * per + i, i * d + r] = 1.0
    return jnp.asarray(m, dtype=BF)


def _qkv(x, mod, ng, wt, cos_t, sin_t, keeps, layer, n_layers, kv_prev, wa, lng, lnb, wm, bias_e, wgo):
    B, S, D = x.shape
    nt = S // TQ
    full = lambda *shape: pl.BlockSpec(shape, lambda b, t: (0,) * len(shape))
    in_specs = [
        pl.BlockSpec((1, TQ, D), lambda b, t: (b, t, 0)),
        pl.BlockSpec((1, 1, 3 * D), lambda b, t: (b, 0, 0)),
        full(1, D), pl.BlockSpec((1, 3 * N_DIL * HW, D), lambda b, t: (layer, 0, 0)),
        pl.BlockSpec((HEAD_DIM // 2, TQ), lambda b, t: (0, t)),
        pl.BlockSpec((HEAD_DIM // 2, TQ), lambda b, t: (0, t)),
        full(PERM, PERM), full(PERM, PERM), full(PERM, PERM),
        pl.BlockSpec((1, D, 4 * D), lambda b, t: (layer, 0, 0)), full(1, D), full(1, D),
        full(GM_GROUPS, CHUNK, CHUNK), full(CHUNK, D), pl.BlockSpec((1, D, D), lambda b, t: (layer, 0, 0)),
    ]
    out_specs, out_shape = [], []
    for _ in range(3):
        for (_, d) in DIL_CONFIGS:
            out_specs.append(pl.BlockSpec((1, d, TQ // d, HW), lambda b, t: (b, 0, t, 0)))
            out_shape.append(jax.ShapeDtypeStruct((B, d, S // d, HW), BF))
    for keep in keeps:
        wcols = min(keep, TQ)
        first = (S - keep) // TQ if keep >= TQ else nt - 1
        out_specs.append(pl.BlockSpec(
            (1, 1, 2, HW, wcols),
            functools.partial(lambda b, t, f: (layer, b, 0, 0, jnp.maximum(t - f, 0)), f=first)))
        out_shape.append(jax.ShapeDtypeStruct((n_layers, B, 2, HW, keep), F32))
    out_specs.append(pl.BlockSpec((1, TQ, D), lambda b, t: (b, t, 0)))
    out_shape.append(jax.ShapeDtypeStruct((B, S, D), BF))
    args = [x, mod, ng, wt, cos_t, sin_t, _perm_t(1), _perm_t(4), _perm_t(16), wa, lng, lnb, wm, bias_e, wgo]
    aliases = {}
    if kv_prev is not None:
        for i, a in enumerate(kv_prev):
            aliases[len(args)] = 3 * N_DIL + i
            in_specs.append(pl.BlockSpec(memory_space=pl.ANY))
            args.append(a)
    n_alias = len(aliases)
    return pl.pallas_call(
        functools.partial(_qkv_kernel, n_alias),
        grid=(B, nt),
        in_specs=in_specs,
        out_specs=out_specs,
        out_shape=out_shape,
        input_output_aliases=aliases,
        compiler_params=_cp(("arbitrary", "arbitrary")),
        name="proj",
    )(*args)


def _band_window(n, qb):
    win = min(2 * BAND, n)
    return max(qb - 1, 0) * BAND if n > win else 0, win


def _band_bias(n, qb):
    start, win = _band_window(n, qb)
    dist = (qb * BAND + np.arange(BAND)[:, None]) - (start + np.arange(win)[None, :])
    return np.where((dist >= 0) & (dist <= BAND), 0.0, NEG).astype(np.float32)


def _band_attn_blocks(n, blocks, q_ref, k_ref, v_ref, b0_ref, b1_ref, o_ref, l_ref):
    lane = lax.broadcasted_iota(jnp.int32, (1, 2 * HEAD_DIM), 1)
    even = lane < HEAD_DIM
    pairs = [slice(p * 2 * HEAD_DIM, (p + 1) * 2 * HEAD_DIM) for p in range(HEADS // 2)]
    ss, vws = [], []
    for rr, qb in blocks:
        start, win = _band_window(n, qb)
        q = q_ref[0, rr, qb * BAND:(qb + 1) * BAND, :]
        kw = k_ref[0, rr, start:start + win, :]
        vws.append(v_ref[0, rr, start:start + win, :])
        bias = b0_ref[...] if qb == 0 else b1_ref[...]
        for sl in pairs:
            qp, kp = q[:, sl], kw[:, sl]
            for sel in (even, ~even):
                ss.append(_dot_nt(jnp.where(sel, qp, jnp.zeros_like(qp)), kp) + bias)
    ms = [jnp.max(s, axis=-1, keepdims=True) for s in ss]
    es = [jnp.exp(s - m).astype(BF) for s, m in zip(ss, ms)]
    for bi, (rr, qb) in enumerate(blocks):
        vw = vws[bi]
        rows = slice(qb * BAND, (qb + 1) * BAND)
        ones = jnp.ones((vw.shape[0], 2 * HEAD_DIM), BF)
        for p, sl in enumerate(pairs):
            i0 = bi * HEADS + 2 * p
            va = jnp.concatenate([vw[:, sl], ones], axis=1)
            r0 = _dot(es[i0], va)
            r1 = _dot(es[i0 + 1], va)
            acc = jnp.where(even, r0[:, :2 * HEAD_DIM], r1[:, :2 * HEAD_DIM])
            den = jnp.where(even, r0[:, 2 * HEAD_DIM:], r1[:, 2 * HEAD_DIM:])
            o_ref[0, rr, rows, sl] = (acc / den).astype(o_ref.dtype)
            l_ref[0, rr, rows, sl] = jnp.where(even, ms[i0], ms[i0 + 1]) + jnp.log(den)


def _band_attn_kernel(n, q_ref, k_ref, v_ref, b0_ref, b1_ref, o_ref, l_ref):
    blocks = [(rr, qb) for rr in range(q_ref.shape[1]) for qb in range(n // BAND)]
    group = ATT_GROUP if n <= BAND else 1
    for i in range(0, len(blocks), group):
        _band_attn_blocks(n, blocks[i:i + group], q_ref, k_ref, v_ref, b0_ref, b1_ref, o_ref, l_ref)


def _band_attn(q, k, v):
    B, d, n, _ = q.shape
    nqb = n // BAND
    nres = max(ATT_BLOCKS // nqb, 1)
    b0 = jnp.asarray(_band_bias(n, 0))
    b1 = jnp.asarray(_band_bias(n, min(1, nqb - 1)))
    blk = pl.BlockSpec((1, nres, n, HW), lambda b, r: (b, r, 0, 0))
    mask = pl.BlockSpec(b0.shape, lambda b, r: (0, 0))
    return pl.pallas_call(
        functools.partial(_band_attn_kernel, n),
        grid=(B, d // nres),
        in_specs=[blk, blk, blk, mask, mask],
        out_specs=[blk, blk],
        out_shape=[jax.ShapeDtypeStruct((B, d, n, HW), BF), jax.ShapeDtypeStruct((B, d, n, HW), F32)],
        compiler_params=_cp(("arbitrary", "arbitrary")),
        name=f"band_attn_d{d}",
    )(q, k, v, b0, b1)


def _final_kernel(last, x_ref, mod_ref, ng_ref, wzg_ref, ya_ref, o0_ref, l0_ref, o1_ref, l1_ref, o2_ref, l2_ref,
                  wao_ref, wo_ref, fg_ref, out_ref, no_ref, nl_ref):
    D = x_ref.shape[-1]
    wzg_ref, wao_ref, wo_ref = wzg_ref.at[0], wao_ref.at[0], wo_ref.at[0]
    mod = mod_ref[0]
    gate = mod[:, 2 * D:3 * D]
    nlt = HW // LANES
    nsub = TS // MERGE_ROWS
    subs = [slice(i * MERGE_ROWS, (i + 1) * MERGE_ROWS) for i in range(nsub)]

    def attention_rows(i):
        outs, lses = [], []
        for g, (o_ref, l_ref) in enumerate(((o0_ref, l0_ref), (o1_ref, l1_ref), (o2_ref, l2_ref))):
            d = o_ref.shape[1]
            per = MERGE_ROWS // d
            src = slice(i * per, (i + 1) * per)
            if d == 1:
                outs.append(o_ref[0, 0, src, :].astype(F32))
                lses.append(l_ref[0, 0, src, :])
                continue
            for r in range(d):
                ov = o_ref[0, r, src, :].astype(F32)
                lv = l_ref[0, r, src, :]
                for c in range(nlt):
                    dst = pl.ds(i * MERGE_ROWS + r, per, stride=d)
                    no_ref[g * nlt + c, dst, :] = ov[:, c * LANES:(c + 1) * LANES]
                    nl_ref[g * nlt + c, dst, :] = lv[:, c * LANES:(c + 1) * LANES]
            outs.append(jnp.concatenate([no_ref[g * nlt + c, subs[i], :] for c in range(nlt)], axis=1))
            lses.append(jnp.concatenate([nl_ref[g * nlt + c, subs[i], :] for c in range(nlt)], axis=1))
        mx = jnp.maximum(jnp.maximum(lses[0], lses[1]), lses[2])
        ws = [jnp.exp(l - mx) for l in lses]
        den = ws[0] + ws[1] + ws[2]
        return (ws[0] * outs[0] + ws[1] * outs[1] + ws[2] * outs[2]) / den

    hs = [_hmod(x_ref[0, sl, :], ng_ref[...], mod, D).astype(BF) for sl in subs]
    zgs = [_dot(h, wzg_ref[...]) for h in hs]
    atts = [attention_rows(i) for i in range(nsub)]
    mbs = []
    for i in range(nsub):
        yb = (atts[i] * _silu(zgs[i][:, 0:HW])).astype(BF)
        mbs.append(_dot(yb, wao_ref[...]))
    for i, sl in enumerate(subs):
        merged = ya_ref[0, sl, :].astype(F32) + _sigmoid(zgs[i][:, HW:HW + D]) * mbs[i]
        xo = x_ref[0, sl, :] + gate * _dot(merged.astype(BF), wo_ref[...])
        if last:
            xo = _rms(xo, fg_ref[...])
        out_ref[0, sl, :] = xo


def _final(layer, last, x, mod, ng, wzg, ya, ols, wao, wo, fg):
    B, S, D = x.shape
    nt = S // TS
    full = lambda *shape: pl.BlockSpec(shape, lambda b, t: (0,) * len(shape))
    tile = pl.BlockSpec((1, TS, D), lambda b, t: (b, t, 0))
    stacked = lambda *shape: pl.BlockSpec((1,) + shape, lambda b, t: (layer,) + (0,) * len(shape))
    in_specs = [tile, pl.BlockSpec((1, 1, 3 * D), lambda b, t: (b, 0, 0)), full(1, D), stacked(D, HW + D), tile]
    args = [x, mod, ng, wzg, ya]
    for (o, l), (_, d) in zip(ols, DIL_CONFIGS):
        spec = pl.BlockSpec((1, d, TS // d, HW), lambda b, t: (b, 0, t, 0))
        in_specs += [spec, spec]
        args += [o, l]
    in_specs += [stacked(HW, D), stacked(D, D), full(1, D)]
    args += [wao, wo, fg]
    return pl.pallas_call(
        functools.partial(_final_kernel, last),
        grid=(B, nt),
        in_specs=in_specs,
        out_specs=tile,
        out_shape=jax.ShapeDtypeStruct((B, S, D), F32),
        scratch_shapes=[pltpu.VMEM((N_DIL * HW // LANES, TS, LANES), F32)] * 2,
        compiler_params=_cp(("arbitrary", "arbitrary")),
        name="merge_out",
    )(*args)


def _sample_proj_kernel(x_ref, mod_ref, ng_ref, w_ref, o_ref):
    D = x_ref.shape[-1]
    h = _hmod(x_ref[...], ng_ref[...], mod_ref[...], D).astype(BF)
    o_ref[...] = _dot(h, w_ref[0].astype(BF))


def _sample_proj(l, x, mod, ng, w_in):
    n, D = x.shape
    N = w_in.shape[-1]
    return pl.pallas_call(
        _sample_proj_kernel,
        grid=(N // D,),
        in_specs=[
            pl.BlockSpec((n, D), lambda j: (0, 0)),
            pl.BlockSpec((n, 3 * D), lambda j: (0, 0)),
            pl.BlockSpec((1, D), lambda j: (0, 0)),
            pl.BlockSpec((1, D, D), lambda j: (l, 0, j)),
        ],
        out_specs=pl.BlockSpec((n, D), lambda j: (0, j)),
        out_shape=jax.ShapeDtypeStruct((n, N), F32),
        compiler_params=_cp(("arbitrary",)),
        name="sample_proj",
    )(x, mod, ng, w_in)


def _rope_nat(x, cos_full, sin_signed):
    lane = lax.broadcasted_iota(jnp.int32, (1, LANES), 1)
    first = (lane & (HEAD_DIM - 1)) < HEAD_DIM // 2
    chunks = []
    for c in range(x.shape[1] // LANES):
        sl = slice(c * LANES, (c + 1) * LANES)
        xc = x[:, sl]
        rot = jnp.where(first, pltpu.roll(xc, LANES - HEAD_DIM // 2, 1), pltpu.roll(xc, HEAD_DIM // 2, 1))
        chunks.append(xc * cos_full[:, sl] + rot * sin_signed[:, sl])
    return jnp.concatenate(chunks, axis=1)


def _sample_mix_kernel(GW, D, p_ref, cos_ref, sin_ref, lng_ref, lnb_ref, coef_ref, bias_ref, c0_ref, c1_ref, c2_ref,
                       ya_ref, yb_ref, sga_ref, sgb_ref, kv0_ref, kv1_ref, kv2_ref, vn_ref):
    T = p_ref.shape[0]
    W3 = N_DIL * HW
    o_q = 3 * GW
    o_zb = o_q + 3 * W3
    o_ga = o_zb + HW
    o_gb = o_ga + D
    u = p_ref[:, 0:GW]
    v = p_ref[:, GW:2 * GW]
    za = p_ref[:, 2 * GW:3 * GW]
    gv = _gelu(v)
    mu = jnp.mean(gv, axis=-1, keepdims=True)
    dv = gv - mu
    var = jnp.mean(dv * dv, axis=-1, keepdims=True)
    vn = dv * lax.rsqrt(var + LN_EPS) * lng_ref[...] + lnb_ref[...]
    vn_ref[...] = vn
    vnb = vn.astype(BF).astype(F32)
    trow = lax.broadcasted_iota(jnp.int32, (T, GW), 0)
    sp = None
    for s in range(T):
        cf = jnp.where(trow >= s, coef_ref[s], 0.0).astype(BF).astype(F32)
        term = cf * vnb[s:s + 1, :]
        sp = term if sp is None else sp + term
    ya_ref[...] = _gelu(u) * (sp + bias_ref[...]) * _silu(za)
    sga_ref[...] = _sigmoid(p_ref[:, o_ga:o_ga + D])
    sgb_ref[...] = _sigmoid(p_ref[:, o_gb:o_gb + D])
    q = _rope_nat(p_ref[:, o_q:o_q + W3], cos_ref[...], sin_ref[...]) * (HEAD_DIM ** -0.5)
    k = _rope_nat(p_ref[:, o_q + W3:o_q + 2 * W3], cos_ref[...], sin_ref[...])
    val = p_ref[:, o_q + 2 * W3:o_q + 3 * W3]
    qb, kb, vb = q.astype(BF), k.astype(BF), val.astype(BF)
    lane = lax.broadcasted_iota(jnp.int32, (1, 2 * HEAD_DIM), 1)
    even = lane < HEAD_DIM
    groups = list(zip(DIL_CONFIGS, (c0_ref, c1_ref, c2_ref), (kv0_ref, kv1_ref, kv2_ref)))
    row_half = jnp.where(lax.broadcasted_iota(jnp.int32, (2 * T, 2 * HEAD_DIM), 0) >= T, 1, 0)
    lane_half = jnp.where(lax.broadcasted_iota(jnp.int32, (2 * T, 2 * HEAD_DIM), 1) >= HEAD_DIM, 1, 0)
    keep2 = row_half == lane_half
    scores = []
    for g, ((win, d), c_ref, kv_ref) in enumerate(groups):
        n = c_ref.shape[-1]
        kv_ref[:, 0:HW] = k[:, g * HW:(g + 1) * HW]
        kv_ref[:, HW:2 * HW] = val[:, g * HW:(g + 1) * HW]
        tq = lax.broadcasted_iota(jnp.int32, (2 * T, n), 0) & (T - 1)
        dist_c = n + tq - lax.broadcasted_iota(jnp.int32, (2 * T, n), 1)
        bias_c = jnp.where((dist_c <= win) & ((dist_c & (d - 1)) == 0), 0.0, NEG)
        dist_n = ((lax.broadcasted_iota(jnp.int32, (2 * T, T), 0) & (T - 1))
                  - lax.broadcasted_iota(jnp.int32, (2 * T, T), 1))
        bias_n = jnp.where((dist_n >= 0) & (dist_n <= win) & ((dist_n & (d - 1)) == 0), 0.0, NEG)
        for p in range(HEADS // 2):
            sl = slice(g * HW + p * 2 * HEAD_DIM, g * HW + (p + 1) * 2 * HEAD_DIM)
            rows = slice(p * 2 * HEAD_DIM, (p + 1) * 2 * HEAD_DIM)
            ktp = c_ref[0, 0, 0, rows, :].astype(BF)
            qp = qb[:, sl]
            q2 = jnp.concatenate([qp, qp], axis=0)
            q2 = jnp.where(keep2, q2, jnp.zeros_like(q2))
            scores.append((_dot(q2, ktp) + bias_c, _dot_nt(q2, kb[:, sl]) + bias_n))
    maxes = [jnp.maximum(jnp.max(s_c, axis=-1, keepdims=True), jnp.max(s_n, axis=-1, keepdims=True))
             for s_c, s_n in scores]
    exps = [(jnp.exp(s_c - m), jnp.exp(s_n - m)) for (s_c, s_n), m in zip(scores, maxes)]
    ms, ls, accs = [], [], []
    i = 0
    for g, ((win, d), c_ref, kv_ref) in enumerate(groups):
        mg, lg, ag = [], [], []
        for p in range(HEADS // 2):
            sl = slice(g * HW + p * 2 * HEAD_DIM, g * HW + (p + 1) * 2 * HEAD_DIM)
            rows = slice(p * 2 * HEAD_DIM, (p + 1) * 2 * HEAD_DIM)
            vtp = c_ref[0, 0, 1, rows, :].astype(BF)
            e_c, e_n = exps[i]
            l2 = jnp.sum(e_c, axis=-1, keepdims=True) + jnp.sum(e_n, axis=-1, keepdims=True)
            acc2 = _dot_nt(e_c.astype(BF), vtp) + _dot(e_n.astype(BF), vb[:, sl])
            mg.append(jnp.where(even, maxes[i][:T], maxes[i][T:]))
            lg.append(jnp.where(even, l2[:T], l2[T:]))
            ag.append(jnp.where(even, acc2[:T], acc2[T:]))
            i += 1
        ms.append(jnp.concatenate(mg, axis=1))
        ls.append(jnp.concatenate(lg, axis=1))
        accs.append(jnp.concatenate(ag, axis=1))
    mx = jnp.maximum(jnp.maximum(ms[0], ms[1]), ms[2])
    ws = [jnp.exp(m - mx) for m in ms]
    num = ws[0] * accs[0] + ws[1] * accs[1] + ws[2] * accs[2]
    den = ws[0] * ls[0] + ws[1] * ls[1] + ws[2] * ls[2]
    yb_ref[...] = (num / den) * _silu(p_ref[:, o_zb:o_zb + HW])


def _sample_mix(l, GW, D, T, proj, cos_s, sin_s, lng, lnb, coef, bias_e, caches):
    n_tok, N = proj.shape
    DB = n_tok // T
    full = lambda *shape: pl.BlockSpec(shape, lambda b: (0,) * len(shape))
    row = lambda w: pl.BlockSpec((T, w), lambda b: (b, 0))
    in_specs = [row(N), full(T, N_DIL * HW), full(T, N_DIL * HW), full(1, GW), full(1, GW), full(T, T, GW),
                full(T, GW)]
    for c in caches:
        in_specs.append(pl.BlockSpec((1, 1, 2, HW, c.shape[-1]), lambda b: (l, b, 0, 0, 0)))
    widths = (GW, HW, D, D, 2 * HW, 2 * HW, 2 * HW, GW)
    return pl.pallas_call(
        functools.partial(_sample_mix_kernel, GW, D),
        grid=(DB,),
        in_specs=in_specs,
        out_specs=[row(w) for w in widths],
        out_shape=[jax.ShapeDtypeStruct((n_tok, w), F32) for w in widths],
        compiler_params=_cp(("arbitrary",)),
        name="sample_mix",
    )(proj, cos_s, sin_s, lng, lnb, coef, bias_e, *caches)


def _sample_out_kernel(last, x_ref, mod_ref, ya_ref, yb_ref, sga_ref, sgb_ref, wgo_ref, wao_ref, wo_ref, fg_ref, o_ref):
    D = x_ref.shape[-1]
    wgo_ref, wao_ref, wo_ref = wgo_ref.at[0], wao_ref.at[0], wo_ref.at[0]
    merged = (sga_ref[...] * _dot(ya_ref[...].astype(BF), wgo_ref[...])
              + sgb_ref[...] * _dot(yb_ref[...].astype(BF), wao_ref[...]))
    xo = x_ref[...] + mod_ref[:, 2 * D:3 * D] * _dot(merged.astype(BF), wo_ref[...])
    if last:
        xo = _rms(xo, fg_ref[...])
    o_ref[...] = xo


def _sample_out(layer, last, x, mod, ya, yb, sga, sgb, wgo, wao, wo, fg):
    args = (x, mod, ya, yb, sga, sgb, wgo, wao, wo, fg)
    whole = lambda a: pl.BlockSpec(a.shape, lambda i, nd=a.ndim: (0,) * nd)
    stacked = lambda a: pl.BlockSpec((1,) + a.shape[1:], lambda i, nd=a.ndim: (layer,) + (0,) * (nd - 1))
    return pl.pallas_call(
        functools.partial(_sample_out_kernel, last),
        grid=(1,),
        in_specs=[whole(a) for a in args[:6]] + [stacked(a) for a in args[6:9]] + [whole(fg)],
        out_specs=pl.BlockSpec(x.shape, lambda i: (0, 0)),
        out_shape=jax.ShapeDtypeStruct(x.shape, F32),
        compiler_params=_cp(("arbitrary",)),
        name="sample_out",
    )(*args)


WCOL = 512


def _regroup_kernel(q_lo, j_zb, j_ga, j_gb, w_ref, wa_ref, wt_ref, wzg_ref):
    j = pl.program_id(1)
    blk = w_ref[0]

    @pl.when((j < q_lo) | ((j >= j_ga) & (j < j_gb)))
    def _():
        wa_ref[0] = blk.astype(BF)

    @pl.when((j >= q_lo) & (j < j_zb))
    def _():
        wt_ref[0] = jnp.transpose(blk).astype(BF)

    @pl.when((j == j_zb) | (j >= j_gb))
    def _():
        wzg_ref[0] = blk.astype(BF)


def _regroup(w_in, GW, D):
    L, _, N = w_in.shape
    W3 = N_DIL * HW
    assert GW % WCOL == 0 and D % WCOL == 0 and HW == WCOL
    n_a = 3 * GW // WCOL
    q_lo, q_hi = n_a, n_a + 3 * W3 // WCOL - 1
    j_zb = q_hi + 1
    j_ga = j_zb + 1
    n_g = D // WCOL
    j_gb = j_ga + n_g
    assert j_gb + n_g == N // WCOL and n_g == n_a // 3
    a_blk = lambda j: jnp.where(j < n_a, j, jnp.where(j < j_ga, n_a - 1, jnp.minimum(j - j_ga, n_g - 1) + n_a))
    t_blk = lambda j: jnp.clip(j - q_lo, 0, q_hi - q_lo)
    z_blk = lambda j: jnp.where(j < j_gb, 0, j - j_gb + 1)
    return pl.pallas_call(
        functools.partial(_regroup_kernel, q_lo, j_zb, j_ga, j_gb),
        grid=(L, N // WCOL),
        in_specs=[pl.BlockSpec((1, D, WCOL), lambda l, j: (l, 0, j))],
        out_specs=[
            pl.BlockSpec((1, D, WCOL), lambda l, j: (l, 0, a_blk(j))),
            pl.BlockSpec((1, WCOL, D), lambda l, j: (l, t_blk(j), 0)),
            pl.BlockSpec((1, D, WCOL), lambda l, j: (l, 0, z_blk(j))),
        ],
        out_shape=[jax.ShapeDtypeStruct((L, D, 3 * GW + D), BF), jax.ShapeDtypeStruct((L, 3 * W3, D), BF),
                   jax.ShapeDtypeStruct((L, D, HW + D), BF)],
        compiler_params=_cp(("arbitrary", "arbitrary")),
        name="regroup_w_in",
    )(w_in)


def kernel(x_prompt, x_sample, cache_kv_w128, cache_kv_w512, cache_kv_w2048, c_prompt, c_sample, w_ada, b_ada, norm_g, w_in, gm_ln_g, gm_ln_b, gm_ws, gm_bs, w_gm_out, w_att_out, w_o, final_g):
    B, S, D = x_prompt.shape
    L = w_in.shape[0]
    W3 = N_DIL * HW
    GW = gm_ln_g.shape[-1]
    assert GW == D and S % TS == 0

    c_all = jnp.concatenate([c_prompt, c_sample], axis=0)
    mod_all = _ada(c_all, w_ada, b_ada)

    o_u, o_q, o_zb, o_ga = 0, 3 * GW, 3 * GW + 3 * W3, 3 * GW + 3 * W3 + HW
    o_gb = o_ga + D
    wa_all, wqkv_t_all, wzg_all = _regroup(w_in, GW, D)
    wgo_all, wao_all, wo_all = w_gm_out.astype(BF), w_att_out.astype(BF), w_o.astype(BF)
    half = HEAD_DIM // 2
    inv_freq = ROPE_THETA ** (-jnp.arange(half, dtype=F32) / half)
    ang_p = jnp.arange(S, dtype=jnp.int32).astype(F32)[:, None] * inv_freq[None, :]
    cos_p, sin_p = jnp.cos(ang_p).T, jnp.sin(ang_p).T

    DB, T, _ = x_sample.shape
    ang_s = (PAST_LEN + jnp.arange(T, dtype=jnp.int32)).astype(F32)[:, None] * inv_freq[None, :]
    cos_s = jnp.tile(jnp.cos(ang_s), (1, 2 * N_DIL * HEADS))
    sin_s = jnp.tile(jnp.concatenate([-jnp.sin(ang_s), jnp.sin(ang_s)], axis=1), (1, N_DIL * HEADS))
    caches_t = []
    for c in (cache_kv_w128, cache_kv_w512, cache_kv_w2048):
        ct = jnp.transpose(c, (0, 1, 3, 4, 5, 2))
        caches_t.append(ct.reshape(L, DB, 2, HW, c.shape[2]))
    xs = x_sample.reshape(DB * T, D)
    kv_s = [[] for _ in DIL_CONFIGS]
    gm_v = []

    keeps = tuple(min(win, S) for win, _ in DIL_CONFIGS)
    xp = x_prompt
    kv_p = None
    for l in range(L):
        mod_p = mod_all[l, :B].reshape(B, 1, 3 * D)
        ng = norm_g[l].reshape(1, D)
        mod_s = jnp.repeat(mod_all[l, B:], T, axis=0)
        proj_s = _sample_proj(l, xs, mod_s, ng, w_in)
        coef = jnp.repeat(jnp.transpose(gm_ws[l][:, :T, :T], (2, 1, 0)), GW // GM_GROUPS, axis=2)
        bias_s = jnp.repeat(gm_bs[l].T[:T], GW // GM_GROUPS, axis=1)
        so = _sample_mix(l, GW, D, T, proj_s, cos_s, sin_s, gm_ln_g[l].reshape(1, GW), gm_ln_b[l].reshape(1, GW),
                         coef, bias_s, caches_t)
        ya_s, yb_s, sga_s, sgb_s = so[0:4]
        xs = _sample_out(l, l == L - 1, xs, mod_s, ya_s, yb_s, sga_s, sgb_s, wgo_all, wao_all, wo_all,
                         final_g.reshape(1, D))
        for g in range(N_DIL):
            kv_s[g].append(so[4 + g].reshape(DB, T, 2, HEADS, HEAD_DIM))
        gm_v.append(so[7].reshape(DB, T, GW))
        bias_e = jnp.repeat(gm_bs[l].T, GW // GM_GROUPS, axis=1)
        outs = _qkv(xp, mod_p, ng, wqkv_t_all, cos_p, sin_p, keeps, l, L, kv_p, wa_all, gm_ln_g[l].reshape(1, GW),
                    gm_ln_b[l].reshape(1, GW), gm_ws[l], bias_e, wgo_all)
        qs, ks, vs, kv_p, ya = outs[0:3], outs[3:6], outs[6:9], outs[9:12], outs[12]
        ols = [_band_attn(qs[g], ks[g], vs[g]) for g in range(N_DIL)]
        xp = _final(l, l == L - 1, xp, mod_p, ng, wzg_all, ya, ols, wao_all, wo_all, final_g.reshape(1, D))

    def kv_out(a):
        a = a.reshape(a.shape[0], a.shape[1], 2, HEADS, HEAD_DIM, a.shape[-1])
        return jnp.transpose(a, (0, 1, 5, 2, 3, 4))

    return (xp, xs.reshape(DB, T, D), kv_out(kv_p[0]), kv_out(kv_p[1]), kv_out(kv_p[2]),
            jnp.stack(kv_s[0]), jnp.stack(kv_s[1]), jnp.stack(kv_s[2]), jnp.stack(gm_v))
```

```python
import functools

import numpy as np
import jax
import jax.numpy as jnp
from jax import lax
from jax.experimental import pallas as pl
from jax.experimental.pallas import tpu as pltpu

GM_GROUPS = 8
CHUNK = 128
HEAD_DIM = 64
HEADS = 8
DIL_CONFIGS = ((128, 1), (512, 4), (2048, 16))
N_DIL = 3
HW = HEADS * HEAD_DIM
BAND = 128
PAST_LEN = 16384
ROPE_THETA = 10000.0
RMS_EPS = 1e-6
LN_EPS = 1e-5
NEG = -1e30

LANES = 128
TS = 512
TQ = 512
PERM = 256
ATT_BLOCKS = 16
ATT_GROUP = 2
MERGE_ROWS = 256
VMEM_LIMIT = 56 * 1024 * 1024

BF = jnp.bfloat16
F32 = jnp.float32


def _cp(sem, flags=None):
    return pltpu.CompilerParams(dimension_semantics=sem, vmem_limit_bytes=VMEM_LIMIT, flags=flags)


def _gelu(x):
    return 0.5 * x * (1.0 + jnp.tanh(0.7978845608028654 * (x + 0.044715 * (x * x * x))))


def _sigmoid(x):
    return 1.0 / (1.0 + jnp.exp(-x))


def _silu(x):
    return x * _sigmoid(x)


def _rms(x, g):
    return x * lax.rsqrt(jnp.mean(x * x, axis=-1, keepdims=True) + RMS_EPS) * g


def _hmod(x, g, mod, d):
    shift = mod[:, 0:d]
    scale = mod[:, d:2 * d]
    return _rms(x, g) * (1.0 + scale) + shift


def _dot(a, b):
    return jnp.dot(a, b, preferred_element_type=F32)


def _dot_nt(a, b):
    return lax.dot_general(a, b, (((1,), (1,)), ((), ())), preferred_element_type=F32)


def _ada_kernel(c_ref, w_ref, b_ref, o_ref):
    c = c_ref[...]
    o_ref[0] = _dot(_silu(c).astype(BF), w_ref[0].astype(BF)) + b_ref[0]


def _ada(c_all, w_ada, b_ada):
    L, D, D3 = w_ada.shape
    n = c_all.shape[0]
    nt = D3 // D
    return pl.pallas_call(
        _ada_kernel,
        grid=(L, nt),
        in_specs=[
            pl.BlockSpec((n, D), lambda l, j: (0, 0)),
            pl.BlockSpec((1, D, D), lambda l, j: (l, 0, j)),
            pl.BlockSpec((1, 1, D), lambda l, j: (l, 0, j)),
        ],
        out_specs=pl.BlockSpec((1, n, D), lambda l, j: (l, 0, j)),
        out_shape=jax.ShapeDtypeStruct((L, n, D3), F32),
        compiler_params=_cp(("arbitrary", "arbitrary")),
        name="ada_mod",
    )(c_all, w_ada, b_ada.reshape(L, 1, D3))


def _branch_a_norm_v(h, wa_ref, lng_ref, lnb_ref):
    D = lng_ref.shape[-1]
    gv = _gelu(_dot(h, wa_ref[:, D:2 * D]))
    mu = jnp.mean(gv, axis=-1, keepdims=True)
    dv = gv - mu
    var = jnp.mean(dv * dv, axis=-1, keepdims=True)
    return (dv * lax.rsqrt(var + LN_EPS) * lng_ref[...] + lnb_ref[...]).astype(BF)


def _branch_a_gates(h, wa_ref):
    D = wa_ref.shape[0]
    return _gelu(_dot(h, wa_ref[:, 0:D])) * _silu(_dot(h, wa_ref[:, 2 * D:3 * D]))


def _branch_a_mix(h, vn, uz, wa_ref, wm_ref, bias_ref, wgo_ref):
    ts = h.shape[0]
    D = wgo_ref.shape[0]
    row = lax.broadcasted_iota(jnp.int32, (CHUNK, CHUNK), 0)
    col = lax.broadcasted_iota(jnp.int32, (CHUNK, CHUNK), 1)
    causal = row >= col
    nch = ts // CHUNK
    cols = []
    for g in range(GM_GROUPS):
        wmg = jnp.where(causal, wm_ref[g], 0.0).astype(BF)
        vg = jnp.concatenate([vn[c * CHUNK:(c + 1) * CHUNK, g * CHUNK:(g + 1) * CHUNK] for c in range(nch)], axis=1)
        mixed = _dot(wmg, vg)
        cols.append(jnp.concatenate([mixed[:, c * CHUNK:(c + 1) * CHUNK] for c in range(nch)], axis=0))
    sp = jnp.concatenate(cols, axis=1)
    bias = jnp.concatenate([bias_ref[...]] * (ts // CHUNK), axis=0)
    ya = (uz * (sp + bias)).astype(BF)
    ga = _dot(h, wa_ref[:, 3 * D:4 * D])
    return (_sigmoid(ga) * _dot(ya, wgo_ref[...])).astype(BF)


def _rope_t(xt, cos, sin):
    half = HEAD_DIM // 2
    pieces = []
    for h in range(xt.shape[0] // HEAD_DIM):
        x1 = xt[h * HEAD_DIM:h * HEAD_DIM + half]
        x2 = xt[h * HEAD_DIM + half:(h + 1) * HEAD_DIM]
        pieces.append(x1 * cos - x2 * sin)
        pieces.append(x2 * cos + x1 * sin)
    return jnp.concatenate(pieces, axis=0)


def _qkv_kernel(n_alias, x_ref, mod_ref, ng_ref, wt_ref, cos_ref, sin_ref, p1_ref, p4_ref, p16_ref,
                wa_ref, lng_ref, lnb_ref, wm_ref, bias_ref, wgo_ref, *refs):
    (q0_ref, q1_ref, q2_ref, k0_ref, k1_ref, k2_ref, v0_ref, v1_ref, v2_ref,
     kv0_ref, kv1_ref, kv2_ref, ya_ref) = refs[n_alias:]
    D = x_ref.shape[-1]
    wt_ref, wa_ref, wgo_ref = wt_ref.at[0], wa_ref.at[0], wgo_ref.at[0]
    h = _hmod(x_ref[0], ng_ref[...], mod_ref[0], D).astype(BF)
    w = N_DIL * HW
    cos = cos_ref[...]
    sin = sin_ref[...]
    p_refs = (p1_ref, p4_ref, p16_ref)
    dst_refs = ((q0_ref, q1_ref, q2_ref), (k0_ref, k1_ref, k2_ref), (v0_ref, v1_ref, v2_ref))
    kv_refs = (kv0_ref, kv1_ref, kv2_ref)
    chunks = [(kind, g) for g in range(N_DIL) for kind in range(3)]

    def project(kind, g):
        r0 = kind * w + g * HW
        return _dot_nt(wt_ref[r0:r0 + HW, :], h)

    vn = _branch_a_norm_v(h, wa_ref, lng_ref, lnb_ref)
    uz = _branch_a_gates(h, wa_ref)
    raw = project(*chunks[0])
    for i, (kind, g) in enumerate(chunks):
        nxt = project(*chunks[i + 1]) if i + 1 < len(chunks) else None
        if kind == 0:
            xt = _rope_t(raw, cos, sin) * (HEAD_DIM ** -0.5)
        elif kind == 1:
            xt = _rope_t(raw, cos, sin)
        else:
            xt = raw
        if kind > 0:
            kv_ref = kv_refs[g]
            kv_ref[0, 0, kind - 1] = xt[:, TQ - kv_ref.shape[-1]:]

        dst = dst_refs[kind][g]
        d = dst.shape[1]
        per = PERM // d
        if d == 1:
            dst[0, 0] = jnp.transpose(xt).astype(BF)
        else:
            xb = xt.astype(BF)
            for hf in range(TQ // PERM):
                nat = _dot_nt(p_refs[g][...], xb[:, hf * PERM:(hf + 1) * PERM]).astype(BF)
                dst[0, :, hf * per:(hf + 1) * per, :] = nat.reshape(d, per, HW)
        raw = nxt
    ya_ref[0] = _branch_a_mix(h, vn, uz, wa_ref, wm_ref, bias_ref, wgo_ref)


def _perm_t(d):
    m = np.zeros((PERM, PERM), np.float32)
    per = PERM // d
    for r in range(d):
        for i in range(per):
            m[r * per + i, i * d + r] = 1.0
    return jnp.asarray(m, dtype=BF)


def _qkv(x, mod, ng, wt, cos_t, sin_t, keeps, layer, n_layers, kv_prev, wa, lng, lnb, wm, bias_e, wgo):
    B, S, D = x.shape
    nt = S // TQ
    full = lambda *shape: pl.BlockSpec(shape, lambda b, t: (0,) * len(shape))
    in_specs = [
        pl.BlockSpec((1, TQ, D), lambda b, t: (b, t, 0)),
        pl.BlockSpec((1, 1, 3 * D), lambda b, t: (b, 0, 0)),
        full(1, D), pl.BlockSpec((1, 3 * N_DIL * HW, D), lambda b, t: (layer, 0, 0)),
        pl.BlockSpec((HEAD_DIM // 2, TQ), lambda b, t: (0, t)),
        pl.BlockSpec((HEAD_DIM // 2, TQ), lambda b, t: (0, t)),
        full(PERM, PERM), full(PERM, PERM), full(PERM, PERM),
        pl.BlockSpec((1, D, 4 * D), lambda b, t: (layer, 0, 0)), full(1, D), full(1, D),
        full(GM_GROUPS, CHUNK, CHUNK), full(CHUNK, D), pl.BlockSpec((1, D, D), lambda b, t: (layer, 0, 0)),
    ]
    out_specs, out_shape = [], []
    for _ in range(3):
        for (_, d) in DIL_CONFIGS:
            out_specs.append(pl.BlockSpec((1, d, TQ // d, HW), lambda b, t: (b, 0, t, 0)))
            out_shape.append(jax.ShapeDtypeStruct((B, d, S // d, HW), BF))
    for keep in keeps:
        wcols = min(keep, TQ)
        first = (S - keep) // TQ if keep >= TQ else nt - 1
        out_specs.append(pl.BlockSpec(
            (1, 1, 2, HW, wcols),
            functools.partial(lambda b, t, f: (layer, b, 0, 0, jnp.maximum(t - f, 0)), f=first)))
        out_shape.append(jax.ShapeDtypeStruct((n_layers, B, 2, HW, keep), F32))
    out_specs.append(pl.BlockSpec((1, TQ, D), lambda b, t: (b, t, 0)))
    out_shape.append(jax.ShapeDtypeStruct((B, S, D), BF))
    args = [x, mod, ng, wt, cos_t, sin_t, _perm_t(1), _perm_t(4), _perm_t(16), wa, lng, lnb, wm, bias_e, wgo]
    aliases = {}
    if kv_prev is not None:
        for i, a in enumerate(kv_prev):
            aliases[len(args)] = 3 * N_DIL + i
            in_specs.append(pl.BlockSpec(memory_space=pl.ANY))
            args.append(a)
    n_alias = len(aliases)
    return pl.pallas_call(
        functools.partial(_qkv_kernel, n_alias),
        grid=(B, nt),
        in_specs=in_specs,
        out_specs=out_specs,
        out_shape=out_shape,
        input_output_aliases=aliases,
        compiler_params=_cp(("arbitrary", "arbitrary")),
        name="proj",
    )(*args)


def _band_window(n, qb):
    win = min(2 * BAND, n)
    return max(qb - 1, 0) * BAND if n > win else 0, win


def _band_bias(n, qb):
    start, win = _band_window(n, qb)
    dist = (qb * BAND + np.arange(BAND)[:, None]) - (start + np.arange(win)[None, :])
    return np.where((dist >= 0) & (dist <= BAND), 0.0, NEG).astype(np.float32)


def _band_attn_blocks(n, blocks, q_ref, k_ref, v_ref, b0_ref, b1_ref, o_ref, l_ref):
    lane = lax.broadcasted_iota(jnp.int32, (1, 2 * HEAD_DIM), 1)
    even = lane < HEAD_DIM
    pairs = [slice(p * 2 * HEAD_DIM, (p + 1) * 2 * HEAD_DIM) for p in range(HEADS // 2)]
    ss, vws = [], []
    for rr, qb in blocks:
        start, win = _band_window(n, qb)
        q = q_ref[0, rr, qb * BAND:(qb + 1) * BAND, :]
        kw = k_ref[0, rr, start:start + win, :]
        vws.append(v_ref[0, rr, start:start + win, :])
        bias = b0_ref[...] if qb == 0 else b1_ref[...]
        for sl in pairs:
            qp, kp = q[:, sl], kw[:, sl]
            for sel in (even, ~even):
                ss.append(_dot_nt(jnp.where(sel, qp, jnp.zeros_like(qp)), kp) + bias)
    ms = [jnp.max(s, axis=-1, keepdims=True) for s in ss]
    es = [jnp.exp(s - m).astype(BF) for s, m in zip(ss, ms)]
    for bi, (rr, qb) in enumerate(blocks):
        vw = vws[bi]
        rows = slice(qb * BAND, (qb + 1) * BAND)
        ones = jnp.ones((vw.shape[0], 2 * HEAD_DIM), BF)
        for p, sl in enumerate(pairs):
            i0 = bi * HEADS + 2 * p
            va = jnp.concatenate([vw[:, sl], ones], axis=1)
            r0 = _dot(es[i0], va)
            r1 = _dot(es[i0 + 1], va)
            acc = jnp.where(even, r0[:, :2 * HEAD_DIM], r1[:, :2 * HEAD_DIM])
            den = jnp.where(even, r0[:, 2 * HEAD_DIM:], r1[:, 2 * HEAD_DIM:])
            o_ref[0, rr, rows, sl] = (acc / den).astype(o_ref.dtype)
            l_ref[0, rr, rows, sl] = jnp.where(even, ms[i0], ms[i0 + 1]) + jnp.log(den)


def _band_attn_kernel(n, q_ref, k_ref, v_ref, b0_ref, b1_ref, o_ref, l_ref):
    blocks = [(rr, qb) for rr in range(q_ref.shape[1]) for qb in range(n // BAND)]
    group = ATT_GROUP if n <= BAND else 1
    for i in range(0, len(blocks), group):
        _band_attn_blocks(n, blocks[i:i + group], q_ref, k_ref, v_ref, b0_ref, b1_ref, o_ref, l_ref)


def _band_attn(q, k, v):
    B, d, n, _ = q.shape
    nqb = n // BAND
    nres = max(ATT_BLOCKS // nqb, 1)
    b0 = jnp.asarray(_band_bias(n, 0))
    b1 = jnp.asarray(_band_bias(n, min(1, nqb - 1)))
    blk = pl.BlockSpec((1, nres, n, HW), lambda b, r: (b, r, 0, 0))
    mask = pl.BlockSpec(b0.shape, lambda b, r: (0, 0))
    return pl.pallas_call(
        functools.partial(_band_attn_kernel, n),
        grid=(B, d // nres),
        in_specs=[blk, blk, blk, mask, mask],
        out_specs=[blk, blk],
        out_shape=[jax.ShapeDtypeStruct((B, d, n, HW), BF), jax.ShapeDtypeStruct((B, d, n, HW), F32)],
        compiler_params=_cp(("arbitrary", "arbitrary")),
        name=f"band_attn_d{d}",
    )(q, k, v, b0, b1)


def _final_kernel(last, x_ref, mod_ref, ng_ref, wzg_ref, ya_ref, o0_ref, l0_ref, o1_ref, l1_ref, o2_ref, l2_ref,
                  wao_ref, wo_ref, fg_ref, out_ref, no_ref, nl_ref):
    D = x_ref.shape[-1]
    wzg_ref, wao_ref, wo_ref = wzg_ref.at[0], wao_ref.at[0], wo_ref.at[0]
    mod = mod_ref[0]
    gate = mod[:, 2 * D:3 * D]
    nlt = HW // LANES
    nsub = TS // MERGE_ROWS
    subs = [slice(i * MERGE_ROWS, (i + 1) * MERGE_ROWS) for i in range(nsub)]

    def attention_rows(i):
        outs, lses = [], []
        for g, (o_ref, l_ref) in enumerate(((o0_ref, l0_ref), (o1_ref, l1_ref), (o2_ref, l2_ref))):
            d = o_ref.shape[1]
            per = MERGE_ROWS // d
            src = slice(i * per, (i + 1) * per)
            if d == 1:
                outs.append(o_ref[0, 0, src, :].astype(F32))
                lses.append(l_ref[0, 0, src, :])
                continue
            for r in range(d):
                ov = o_ref[0, r, src, :].astype(F32)
                lv = l_ref[0, r, src, :]
                for c in range(nlt):
                    dst = pl.ds(i * MERGE_ROWS + r, per, stride=d)
                    no_ref[g * nlt + c, dst, :] = ov[:, c * LANES:(c + 1) * LANES]
                    nl_ref[g * nlt + c, dst, :] = lv[:, c * LANES:(c + 1) * LANES]
            outs.append(jnp.concatenate([no_ref[g * nlt + c, subs[i], :] for c in range(nlt)], axis=1))
            lses.append(jnp.concatenate([nl_ref[g * nlt + c, subs[i], :] for c in range(nlt)], axis=1))
        mx = jnp.maximum(jnp.maximum(lses[0], lses[1]), lses[2])
        ws = [jnp.exp(l - mx) for l in lses]
        den = ws[0] + ws[1] + ws[2]
        return (ws[0] * outs[0] + ws[1] * outs[1] + ws[2] * outs[2]) / den

    hs = [_hmod(x_ref[0, sl, :], ng_ref[...], mod, D).astype(BF) for sl in subs]
    zgs = [_dot(h, wzg_ref[...]) for h in hs]
    atts = [attention_rows(i) for i in range(nsub)]
    mbs = []
    for i in range(nsub):
        yb = (atts[i] * _silu(zgs[i][:, 0:HW])).astype(BF)
        mbs.append(_dot(yb, wao_ref[...]))
    for i, sl in enumerate(subs):
        merged = ya_ref[0, sl, :].astype(F32) + _sigmoid(zgs[i][:, HW:HW + D]) * mbs[i]
        xo = x_ref[0, sl, :] + gate * _dot(merged.astype(BF), wo_ref[...])
        if last:
            xo = _rms(xo, fg_ref[...])
        out_ref[0, sl, :] = xo


def _final(layer, last, x, mod, ng, wzg, ya, ols, wao, wo, fg):
    B, S, D = x.shape
    nt = S // TS
    full = lambda *shape: pl.BlockSpec(shape, lambda b, t: (0,) * len(shape))
    tile = pl.BlockSpec((1, TS, D), lambda b, t: (b, t, 0))
    stacked = lambda *shape: pl.BlockSpec((1,) + shape, lambda b, t: (layer,) + (0,) * len(shape))
    in_specs = [tile, pl.BlockSpec((1, 1, 3 * D), lambda b, t: (b, 0, 0)), full(1, D), stacked(D, HW + D), tile]
    args = [x, mod, ng, wzg, ya]
    for (o, l), (_, d) in zip(ols, DIL_CONFIGS):
        spec = pl.BlockSpec((1, d, TS // d, HW), lambda b, t: (b, 0, t, 0))
        in_specs += [spec, spec]
        args += [o, l]
    in_specs += [stacked(HW, D), stacked(D, D), full(1, D)]
    args += [wao, wo, fg]
    return pl.pallas_call(
        functools.partial(_final_kernel, last),
        grid=(B, nt),
        in_specs=in_specs,
        out_specs=tile,
        out_shape=jax.ShapeDtypeStruct((B, S, D), F32),
        scratch_shapes=[pltpu.VMEM((N_DIL * HW // LANES, TS, LANES), F32)] * 2,
        compiler_params=_cp(("arbitrary", "arbitrary")),
        name="merge_out",
    )(*args)


def _sample_proj_kernel(x_ref, mod_ref, ng_ref, w_ref, o_ref):
    D = x_ref.shape[-1]
    h = _hmod(x_ref[...], ng_ref[...], mod_ref[...], D).astype(BF)
    o_ref[...] = _dot(h, w_ref[0].astype(BF))


def _sample_proj(l, x, mod, ng, w_in):
    n, D = x.shape
    N = w_in.shape[-1]
    return pl.pallas_call(
        _sample_proj_kernel,
        grid=(N // D,),
        in_specs=[
            pl.BlockSpec((n, D), lambda j: (0, 0)),
            pl.BlockSpec((n, 3 * D), lambda j: (0, 0)),
            pl.BlockSpec((1, D), lambda j: (0, 0)),
            pl.BlockSpec((1, D, D), lambda j: (l, 0, j)),
        ],
        out_specs=pl.BlockSpec((n, D), lambda j: (0, j)),
        out_shape=jax.ShapeDtypeStruct((n, N), F32),
        compiler_params=_cp(("arbitrary",)),
        name="sample_proj",
    )(x, mod, ng, w_in)


def _rope_nat(x, cos_full, sin_signed):
    lane = lax.broadcasted_iota(jnp.int32, (1, LANES), 1)
    first = (lane & (HEAD_DIM - 1)) < HEAD_DIM // 2
    chunks = []
    for c in range(x.shape[1] // LANES):
        sl = slice(c * LANES, (c + 1) * LANES)
        xc = x[:, sl]
        rot = jnp.where(first, pltpu.roll(xc, LANES - HEAD_DIM // 2, 1), pltpu.roll(xc, HEAD_DIM // 2, 1))
        chunks.append(xc * cos_full[:, sl] + rot * sin_signed[:, sl])
    return jnp.concatenate(chunks, axis=1)


def _sample_mix_kernel(GW, D, p_ref, cos_ref, sin_ref, lng_ref, lnb_ref, coef_ref, bias_ref, c0_ref, c1_ref, c2_ref,
                       ya_ref, yb_ref, sga_ref, sgb_ref, kv0_ref, kv1_ref, kv2_ref, vn_ref):
    T = p_ref.shape[0]
    W3 = N_DIL * HW
    o_q = 3 * GW
    o_zb = o_q + 3 * W3
    o_ga = o_zb + HW
    o_gb = o_ga + D
    u = p_ref[:, 0:GW]
    v = p_ref[:, GW:2 * GW]
    za = p_ref[:, 2 * GW:3 * GW]
    gv = _gelu(v)
    mu = jnp.mean(gv, axis=-1, keepdims=True)
    dv = gv - mu
    var = jnp.mean(dv * dv, axis=-1, keepdims=True)
    vn = dv * lax.rsqrt(var + LN_EPS) * lng_ref[...] + lnb_ref[...]
    vn_ref[...] = vn
    vnb = vn.astype(BF).astype(F32)
    trow = lax.broadcasted_iota(jnp.int32, (T, GW), 0)
    sp = None
    for s in range(T):
        cf = jnp.where(trow >= s, coef_ref[s], 0.0).astype(BF).astype(F32)
        term = cf * vnb[s:s + 1, :]
        sp = term if sp is None else sp + term
    ya_ref[...] = _gelu(u) * (sp + bias_ref[...]) * _silu(za)
    sga_ref[...] = _sigmoid(p_ref[:, o_ga:o_ga + D])
    sgb_ref[...] = _sigmoid(p_ref[:, o_gb:o_gb + D])
    q = _rope_nat(p_ref[:, o_q:o_q + W3], cos_ref[...], sin_ref[...]) * (HEAD_DIM ** -0.5)
    k = _rope_nat(p_ref[:, o_q + W3:o_q + 2 * W3], cos_ref[...], sin_ref[...])
    val = p_ref[:, o_q + 2 * W3:o_q + 3 * W3]
    qb, kb, vb = q.astype(BF), k.astype(BF), val.astype(BF)
    lane = lax.broadcasted_iota(jnp.int32, (1, 2 * HEAD_DIM), 1)
    even = lane < HEAD_DIM
    groups = list(zip(DIL_CONFIGS, (c0_ref, c1_ref, c2_ref), (kv0_ref, kv1_ref, kv2_ref)))
    row_half = jnp.where(lax.broadcasted_iota(jnp.int32, (2 * T, 2 * HEAD_DIM), 0) >= T, 1, 0)
    lane_half = jnp.where(lax.broadcasted_iota(jnp.int32, (2 * T, 2 * HEAD_DIM), 1) >= HEAD_DIM, 1, 0)
    keep2 = row_half == lane_half
    scores = []
    for g, ((win, d), c_ref, kv_ref) in enumerate(groups):
        n = c_ref.shape[-1]
        kv_ref[:, 0:HW] = k[:, g * HW:(g + 1) * HW]
        kv_ref[:, HW:2 * HW] = val[:, g * HW:(g + 1) * HW]
        tq = lax.broadcasted_iota(jnp.int32, (2 * T, n), 0) & (T - 1)
        dist_c = n + tq - lax.broadcasted_iota(jnp.int32, (2 * T, n), 1)
        bias_c = jnp.where((dist_c <= win) & ((dist_c & (d - 1)) == 0), 0.0, NEG)
        dist_n = ((lax.broadcasted_iota(jnp.int32, (2 * T, T), 0) & (T - 1))
                  - lax.broadcasted_iota(jnp.int32, (2 * T, T), 1))
        bias_n = jnp.where((dist_n >= 0) & (dist_n <= win) & ((dist_n & (d - 1)) == 0), 0.0, NEG)
        for p in range(HEADS // 2):
            sl = slice(g * HW + p * 2 * HEAD_DIM, g * HW + (p + 1) * 2 * HEAD_DIM)
            rows = slice(p * 2 * HEAD_DIM, (p + 1) * 2 * HEAD_DIM)
            ktp = c_ref[0, 0, 0, rows, :].astype(BF)
            qp = qb[:, sl]
            q2 = jnp.concatenate([qp, qp], axis=0)
            q2 = jnp.where(keep2, q2, jnp.zeros_like(q2))
            scores.append((_dot(q2, ktp) + bias_c, _dot_nt(q2, kb[:, sl]) + bias_n))
    maxes = [jnp.maximum(jnp.max(s_c, axis=-1, keepdims=True), jnp.max(s_n, axis=-1, keepdims=True))
             for s_c, s_n in scores]
    exps = [(jnp.exp(s_c - m), jnp.exp(s_n - m)) for (s_c, s_n), m in zip(scores, maxes)]
    ms, ls, accs = [], [], []
    i = 0
    for g, ((win, d), c_ref, kv_ref) in enumerate(groups):
        mg, lg, ag = [], [], []
        for p in range(HEADS // 2):
            sl = slice(g * HW + p * 2 * HEAD_DIM, g * HW + (p + 1) * 2 * HEAD_DIM)
            rows = slice(p * 2 * HEAD_DIM, (p + 1) * 2 * HEAD_DIM)
            vtp = c_ref[0, 0, 1, rows, :].astype(BF)
            e_c, e_n = exps[i]
            l2 = jnp.sum(e_c, axis=-1, keepdims=True) + jnp.sum(e_n, axis=-1, keepdims=True)
            acc2 = _dot_nt(e_c.astype(BF), vtp) + _dot(e_n.astype(BF), vb[:, sl])
            mg.append(jnp.where(even, maxes[i][:T], maxes[i][T:]))
            lg.append(jnp.where(even, l2[:T], l2[T:]))
            ag.append(jnp.where(even, acc2[:T], acc2[T:]))
            i += 1
        ms.append(jnp.concatenate(mg, axis=1))
        ls.append(jnp.concatenate(lg, axis=1))
        accs.append(jnp.concatenate(ag, axis=1))
    mx = jnp.maximum(jnp.maximum(ms[0], ms[1]), ms[2])
    ws = [jnp.exp(m - mx) for m in ms]
    num = ws[0] * accs[0] + ws[1] * accs[1] + ws[2] * accs[2]
    den = ws[0] * ls[0] + ws[1] * ls[1] + ws[2] * ls[2]
    yb_ref[...] = (num / den) * _silu(p_ref[:, o_zb:o_zb + HW])


N_SAMPLE_IN, N_SAMPLE_OUT = 10, 8


def _attn_sample_kernel(n, GW, D, *refs):
    q_ref, k_ref, v_ref, b0_ref, b1_ref = refs[0:5]
    s_in = refs[5:5 + N_SAMPLE_IN]
    o_ref, l_ref = refs[5 + N_SAMPLE_IN:7 + N_SAMPLE_IN]
    s_out = refs[7 + N_SAMPLE_IN:]
    _band_attn_kernel(n, q_ref, k_ref, v_ref, b0_ref, b1_ref, o_ref, l_ref)
    _sample_mix_kernel(GW, D, *s_in, *s_out)


def _band_attn_sample(q, k, v, l, GW, D, T, proj, cos_s, sin_s, lng, lnb, coef, bias_e, caches):
    B, d, n, _ = q.shape
    n_tok, N = proj.shape
    DB = n_tok // T
    split = DB // B
    assert DB == B * split and d % split == 0 and len(caches) == N_DIL
    nres = d // split
    nqb = n // BAND
    b0 = jnp.asarray(_band_bias(n, 0))
    b1 = jnp.asarray(_band_bias(n, min(1, nqb - 1)))
    blk = pl.BlockSpec((1, nres, n, HW), lambda b, h: (b, h, 0, 0))
    full = lambda *shape: pl.BlockSpec(shape, lambda b, h: (0,) * len(shape))
    row = lambda w: pl.BlockSpec((T, w), lambda b, h: (b * split + h, 0))
    in_specs = [blk, blk, blk, full(*b0.shape), full(*b1.shape),
                row(N), full(T, N_DIL * HW), full(T, N_DIL * HW), full(1, GW), full(1, GW), full(T, T, GW), full(T, GW)]
    for c in caches:
        in_specs.append(pl.BlockSpec((1, 1, 2, HW, c.shape[-1]), lambda b, h: (l, b * split + h, 0, 0, 0)))
    widths = (GW, HW, D, D, 2 * HW, 2 * HW, 2 * HW, GW)
    assert len(in_specs) == 5 + N_SAMPLE_IN and len(widths) == N_SAMPLE_OUT
    outs = pl.pallas_call(
        functools.partial(_attn_sample_kernel, n, GW, D),
        grid=(B, split),
        in_specs=in_specs,
        out_specs=[blk, blk] + [row(w) for w in widths],
        out_shape=[jax.ShapeDtypeStruct((B, d, n, HW), BF), jax.ShapeDtypeStruct((B, d, n, HW), F32)]
        + [jax.ShapeDtypeStruct((n_tok, w), F32) for w in widths],
        compiler_params=_cp(("arbitrary", "arbitrary")),
        name=f"band_attn_d{d}_sample_mix",
    )(q, k, v, b0, b1, proj, cos_s, sin_s, lng, lnb, coef, bias_e, *caches)
    return outs[0:2], outs[2:]


def _sample_out_kernel(last, x_ref, mod_ref, ya_ref, yb_ref, sga_ref, sgb_ref, wgo_ref, wao_ref, wo_ref, fg_ref, o_ref):
    D = x_ref.shape[-1]
    wgo_ref, wao_ref, wo_ref = wgo_ref.at[0], wao_ref.at[0], wo_ref.at[0]
    merged = (sga_ref[...] * _dot(ya_ref[...].astype(BF), wgo_ref[...])
              + sgb_ref[...] * _dot(yb_ref[...].astype(BF), wao_ref[...]))
    xo = x_ref[...] + mod_ref[:, 2 * D:3 * D] * _dot(merged.astype(BF), wo_ref[...])
    if last:
        xo = _rms(xo, fg_ref[...])
    o_ref[...] = xo


def _sample_out(layer, last, x, mod, ya, yb, sga, sgb, wgo, wao, wo, fg):
    args = (x, mod, ya, yb, sga, sgb, wgo, wao, wo, fg)
    whole = lambda a: pl.BlockSpec(a.shape, lambda i, nd=a.ndim: (0,) * nd)
    stacked = lambda a: pl.BlockSpec((1,) + a.shape[1:], lambda i, nd=a.ndim: (layer,) + (0,) * (nd - 1))
    return pl.pallas_call(
        functools.partial(_sample_out_kernel, last),
        grid=(1,),
        in_specs=[whole(a) for a in args[:6]] + [stacked(a) for a in args[6:9]] + [whole(fg)],
        out_specs=pl.BlockSpec(x.shape, lambda i: (0, 0)),
        out_shape=jax.ShapeDtypeStruct(x.shape, F32),
        compiler_params=_cp(("arbitrary",)),
        name="sample_out",
    )(*args)


WCOL = 512


def _regroup_kernel(q_lo, j_zb, j_ga, j_gb, w_ref, wa_ref, wt_ref, wzg_ref):
    j = pl.program_id(1)
    blk = w_ref[0]

    @pl.when((j < q_lo) | ((j >= j_ga) & (j < j_gb)))
    def _():
        wa_ref[0] = blk.astype(BF)

    @pl.when((j >= q_lo) & (j < j_zb))
    def _():
        wt_ref[0] = jnp.transpose(blk).astype(BF)

    @pl.when((j == j_zb) | (j >= j_gb))
    def _():
        wzg_ref[0] = blk.astype(BF)


def _regroup(w_in, GW, D):
    L, _, N = w_in.shape
    W3 = N_DIL * HW
    assert GW % WCOL == 0 and D % WCOL == 0 and HW == WCOL
    n_a = 3 * GW // WCOL
    q_lo, q_hi = n_a, n_a + 3 * W3 // WCOL - 1
    j_zb = q_hi + 1
    j_ga = j_zb + 1
    n_g = D // WCOL
    j_gb = j_ga + n_g
    assert j_gb + n_g == N // WCOL and n_g == n_a // 3
    a_blk = lambda j: jnp.where(j < n_a, j, jnp.where(j < j_ga, n_a - 1, jnp.minimum(j - j_ga, n_g - 1) + n_a))
    t_blk = lambda j: jnp.clip(j - q_lo, 0, q_hi - q_lo)
    z_blk = lambda j: jnp.where(j < j_gb, 0, j - j_gb + 1)
    return pl.pallas_call(
        functools.partial(_regroup_kernel, q_lo, j_zb, j_ga, j_gb),
        grid=(L, N // WCOL),
        in_specs=[pl.BlockSpec((1, D, WCOL), lambda l, j: (l, 0, j))],
        out_specs=[
            pl.BlockSpec((1, D, WCOL), lambda l, j: (l, 0, a_blk(j))),
            pl.BlockSpec((1, WCOL, D), lambda l, j: (l, t_blk(j), 0)),
            pl.BlockSpec((1, D, WCOL), lambda l, j: (l, 0, z_blk(j))),
        ],
        out_shape=[jax.ShapeDtypeStruct((L, D, 3 * GW + D), BF), jax.ShapeDtypeStruct((L, 3 * W3, D), BF),
                   jax.ShapeDtypeStruct((L, D, HW + D), BF)],
        compiler_params=_cp(("arbitrary", "arbitrary")),
        name="regroup_w_in",
    )(w_in)


def kernel(x_prompt, x_sample, cache_kv_w128, cache_kv_w512, cache_kv_w2048, c_prompt, c_sample, w_ada, b_ada, norm_g, w_in, gm_ln_g, gm_ln_b, gm_ws, gm_bs, w_gm_out, w_att_out, w_o, final_g):
    B, S, D = x_prompt.shape
    L = w_in.shape[0]
    W3 = N_DIL * HW
    GW = gm_ln_g.shape[-1]
    assert GW == D and S % TS == 0

    c_all = jnp.concatenate([c_prompt, c_sample], axis=0)
    mod_all = _ada(c_all, w_ada, b_ada)

    o_u, o_q, o_zb, o_ga = 0, 3 * GW, 3 * GW + 3 * W3, 3 * GW + 3 * W3 + HW
    o_gb = o_ga + D
    wa_all, wqkv_t_all, wzg_all = _regroup(w_in, GW, D)
    wgo_all, wao_all, wo_all = w_gm_out.astype(BF), w_att_out.astype(BF), w_o.astype(BF)
    half = HEAD_DIM // 2
    inv_freq = ROPE_THETA ** (-jnp.arange(half, dtype=F32) / half)
    ang_p = jnp.arange(S, dtype=jnp.int32).astype(F32)[:, None] * inv_freq[None, :]
    cos_p, sin_p = jnp.cos(ang_p).T, jnp.sin(ang_p).T

    DB, T, _ = x_sample.shape
    ang_s = (PAST_LEN + jnp.arange(T, dtype=jnp.int32)).astype(F32)[:, None] * inv_freq[None, :]
    cos_s = jnp.tile(jnp.cos(ang_s), (1, 2 * N_DIL * HEADS))
    sin_s = jnp.tile(jnp.concatenate([-jnp.sin(ang_s), jnp.sin(ang_s)], axis=1), (1, N_DIL * HEADS))
    caches_t = []
    for c in (cache_kv_w128, cache_kv_w512, cache_kv_w2048):
        ct = jnp.transpose(c, (0, 1, 3, 4, 5, 2))
        caches_t.append(ct.reshape(L, DB, 2, HW, c.shape[2]))
    xs = x_sample.reshape(DB * T, D)
    kv_s = [[] for _ in DIL_CONFIGS]
    gm_v = []

    keeps = tuple(min(win, S) for win, _ in DIL_CONFIGS)
    xp = x_prompt
    kv_p = None
    for l in range(L):
        mod_p = mod_all[l, :B].reshape(B, 1, 3 * D)
        ng = norm_g[l].reshape(1, D)
        mod_s = jnp.repeat(mod_all[l, B:], T, axis=0)
        proj_s = _sample_proj(l, xs, mod_s, ng, w_in)
        coef = jnp.repeat(jnp.transpose(gm_ws[l][:, :T, :T], (2, 1, 0)), GW // GM_GROUPS, axis=2)
        bias_s = jnp.repeat(gm_bs[l].T[:T], GW // GM_GROUPS, axis=1)
        bias_e = jnp.repeat(gm_bs[l].T, GW // GM_GROUPS, axis=1)
        outs = _qkv(xp, mod_p, ng, wqkv_t_all, cos_p, sin_p, keeps, l, L, kv_p, wa_all, gm_ln_g[l].reshape(1, GW),
                    gm_ln_b[l].reshape(1, GW), gm_ws[l], bias_e, wgo_all)
        qs, ks, vs, kv_p, ya = outs[0:3], outs[3:6], outs[6:9], outs[9:12], outs[12]
        ols = [_band_attn(qs[g], ks[g], vs[g]) for g in range(N_DIL - 1)]
        ol_last, so = _band_attn_sample(qs[-1], ks[-1], vs[-1], l, GW, D, T, proj_s, cos_s, sin_s,
                                        gm_ln_g[l].reshape(1, GW), gm_ln_b[l].reshape(1, GW), coef, bias_s, caches_t)
        ols.append(ol_last)
        ya_s, yb_s, sga_s, sgb_s = so[0:4]
        xs = _sample_out(l, l == L - 1, xs, mod_s, ya_s, yb_s, sga_s, sgb_s, wgo_all, wao_all, wo_all,
                         final_g.reshape(1, D))
        for g in range(N_DIL):
            kv_s[g].append(so[4 + g].reshape(DB, T, 2, HEADS, HEAD_DIM))
        gm_v.append(so[7].reshape(DB, T, GW))
        xp = _final(l, l == L - 1, xp, mod_p, ng, wzg_all, ya, ols, wao_all, wo_all, final_g.reshape(1, D))

    def kv_out(a):
        a = a.reshape(a.shape[0], a.shape[1], 2, HEADS, HEAD_DIM, a.shape[-1])
        return jnp.transpose(a, (0, 1, 5, 2, 3, 4))

    return (xp, xs.reshape(DB, T, D), kv_out(kv_p[0]), kv_out(kv_p[1]), kv_out(kv_p[2]),
            jnp.stack(kv_s[0]), jnp.stack(kv_s[1]), jnp.stack(kv_s[2]), jnp.stack(gm_v))
```

```python
import functools

import numpy as np
import jax
import jax.numpy as jnp
from jax import lax
from jax.experimental import pallas as pl
from jax.experimental.pallas import tpu as pltpu

GM_GROUPS = 8
CHUNK = 128
HEAD_DIM = 64
HEADS = 8
DIL_CONFIGS = ((128, 1), (512, 4), (2048, 16))
N_DIL = 3
HW = HEADS * HEAD_DIM
BAND = 128
PAST_LEN = 16384
ROPE_THETA = 10000.0
RMS_EPS = 1e-6
LN_EPS = 1e-5
NEG = -1e30

LANES = 128
TS = 512
TQ = 512
PERM = 256
ATT_BLOCKS = 16
ATT_GROUP = 2
MERGE_ROWS = 256
VMEM_LIMIT = 56 * 1024 * 1024

BF = jnp.bfloat16
F32 = jnp.float32


def _cp(sem, flags=None):
    return pltpu.CompilerParams(dimension_semantics=sem, vmem_limit_bytes=VMEM_LIMIT, flags=flags)


def _gelu(x):
    return 0.5 * x * (1.0 + jnp.tanh(0.7978845608028654 * (x + 0.044715 * (x * x * x))))


def _sigmoid(x):
    return 1.0 / (1.0 + jnp.exp(-x))


def _silu(x):
    return x * _sigmoid(x)


def _rms(x, g):
    return x * lax.rsqrt(jnp.mean(x * x, axis=-1, keepdims=True) + RMS_EPS) * g


def _hmod(x, g, mod, d):
    shift = mod[:, 0:d]
    scale = mod[:, d:2 * d]
    return _rms(x, g) * (1.0 + scale) + shift


def _dot(a, b):
    return jnp.dot(a, b, preferred_element_type=F32)


def _dot_nt(a, b):
    return lax.dot_general(a, b, (((1,), (1,)), ((), ())), preferred_element_type=F32)


def _ada_kernel(c_ref, w_ref, b_ref, o_ref):
    c = c_ref[...]
    o_ref[0] = _dot(_silu(c).astype(BF), w_ref[0].astype(BF)) + b_ref[0]


def _ada(c_all, w_ada, b_ada):
    L, D, D3 = w_ada.shape
    n = c_all.shape[0]
    nt = D3 // D
    return pl.pallas_call(
        _ada_kernel,
        grid=(L, nt),
        in_specs=[
            pl.BlockSpec((n, D), lambda l, j: (0, 0)),
            pl.BlockSpec((1, D, D), lambda l, j: (l, 0, j)),
            pl.BlockSpec((1, 1, D), lambda l, j: (l, 0, j)),
        ],
        out_specs=pl.BlockSpec((1, n, D), lambda l, j: (l, 0, j)),
        out_shape=jax.ShapeDtypeStruct((L, n, D3), F32),
        compiler_params=_cp(("arbitrary", "arbitrary")),
        name="ada_mod",
    )(c_all, w_ada, b_ada.reshape(L, 1, D3))


def _branch_a_norm_v(h, wa_ref, lng_ref, lnb_ref):
    D = lng_ref.shape[-1]
    gv = _gelu(_dot(h, wa_ref[:, D:2 * D]))
    mu = jnp.mean(gv, axis=-1, keepdims=True)
    dv = gv - mu
    var = jnp.mean(dv * dv, axis=-1, keepdims=True)
    return (dv * lax.rsqrt(var + LN_EPS) * lng_ref[...] + lnb_ref[...]).astype(BF)


def _branch_a_gates(h, wa_ref):
    D = wa_ref.shape[0]
    return _gelu(_dot(h, wa_ref[:, 0:D])) * _silu(_dot(h, wa_ref[:, 2 * D:3 * D]))


def _branch_a_mix(h, vn, uz, wa_ref, wm_ref, bias_ref, wgo_ref):
    ts = h.shape[0]
    D = wgo_ref.shape[0]
    row = lax.broadcasted_iota(jnp.int32, (CHUNK, CHUNK), 0)
    col = lax.broadcasted_iota(jnp.int32, (CHUNK, CHUNK), 1)
    causal = row >= col
    nch = ts // CHUNK
    cols = []
    for g in range(GM_GROUPS):
        wmg = jnp.where(causal, wm_ref[g], 0.0).astype(BF)
        vg = jnp.concatenate([vn[c * CHUNK:(c + 1) * CHUNK, g * CHUNK:(g + 1) * CHUNK] for c in range(nch)], axis=1)
        mixed = _dot(wmg, vg)
        cols.append(jnp.concatenate([mixed[:, c * CHUNK:(c + 1) * CHUNK] for c in range(nch)], axis=0))
    sp = jnp.concatenate(cols, axis=1)
    bias = jnp.concatenate([bias_ref[...]] * (ts // CHUNK), axis=0)
    ya = (uz * (sp + bias)).astype(BF)
    ga = _dot(h, wa_ref[:, 3 * D:4 * D])
    return (_sigmoid(ga) * _dot(ya, wgo_ref[...])).astype(BF)


def _rope_t(xt, cos, sin):
    half = HEAD_DIM // 2
    pieces = []
    for h in range(xt.shape[0] // HEAD_DIM):
        x1 = xt[h * HEAD_DIM:h * HEAD_DIM + half]
        x2 = xt[h * HEAD_DIM + half:(h + 1) * HEAD_DIM]
        pieces.append(x1 * cos - x2 * sin)
        pieces.append(x2 * cos + x1 * sin)
    return jnp.concatenate(pieces, axis=0)


def _qkv_kernel(n_alias, x_ref, mod_ref, ng_ref, wt_ref, cos_ref, sin_ref, p1_ref, p4_ref, p16_ref,
                wa_ref, lng_ref, lnb_ref, wm_ref, bias_ref, wgo_ref, *refs):
    (q0_ref, q1_ref, q2_ref, k0_ref, k1_ref, k2_ref, v0_ref, v1_ref, v2_ref,
     kv0_ref, kv1_ref, kv2_ref, ya_ref) = refs[n_alias:]
    D = x_ref.shape[-1]
    wt_ref, wa_ref, wgo_ref = wt_ref.at[0], wa_ref.at[0], wgo_ref.at[0]
    h = _hmod(x_ref[0], ng_ref[...], mod_ref[0], D).astype(BF)
    w = N_DIL * HW
    cos = cos_ref[...]
    sin = sin_ref[...]
    p_refs = (p1_ref, p4_ref, p16_ref)
    dst_refs = ((q0_ref, q1_ref, q2_ref), (k0_ref, k1_ref, k2_ref), (v0_ref, v1_ref, v2_ref))
    kv_refs = (kv0_ref, kv1_ref, kv2_ref)
    chunks = [(kind, g) for g in range(N_DIL) for kind in range(3)]

    def project(kind, g):
        r0 = kind * w + g * HW
        return _dot_nt(wt_ref[r0:r0 + HW, :], h)

    vn = _branch_a_norm_v(h, wa_ref, lng_ref, lnb_ref)
    uz = _branch_a_gates(h, wa_ref)
    raw = project(*chunks[0])
    for i, (kind, g) in enumerate(chunks):
        nxt = project(*chunks[i + 1]) if i + 1 < len(chunks) else None
        if kind == 0:
            xt = _rope_t(raw, cos, sin) * (HEAD_DIM ** -0.5)
        elif kind == 1:
            xt = _rope_t(raw, cos, sin)
        else:
            xt = raw
        if kind > 0:
            kv_ref = kv_refs[g]
            kv_ref[0, 0, kind - 1] = xt[:, TQ - kv_ref.shape[-1]:]

        dst = dst_refs[kind][g]
        d = dst.shape[1]
        per = PERM // d
        if d == 1:
            dst[0, 0] = jnp.transpose(xt).astype(BF)
        else:
            xb = xt.astype(BF)
            for hf in range(TQ // PERM):
                nat = _dot_nt(p_refs[g][...], xb[:, hf * PERM:(hf + 1) * PERM]).astype(BF)
                dst[0, :, hf * per:(hf + 1) * per, :] = nat.reshape(d, per, HW)
        raw = nxt
    ya_ref[0] = _branch_a_mix(h, vn, uz, wa_ref, wm_ref, bias_ref, wgo_ref)


def _perm_t(d):
    m = np.zeros((PERM, PERM), np.float32)
    per = PERM // d
    for r in range(d):
        for i in range(per):
            m[r * per + i, i * d + r] = 1.0
    return jnp.asarray(m, dtype=BF)


def _qkv(x, mod, ng, wt, cos_t, sin_t, keeps, layer, n_layers, kv_prev, wa, lng, lnb, wm, bias_e, wgo):
    B, S, D = x.shape
    nt = S // TQ
    full = lambda *shape: pl.BlockSpec(shape, lambda b, t: (0,) * len(shape))
    in_specs = [
        pl.BlockSpec((1, TQ, D), lambda b, t: (b, t, 0)),
        pl.BlockSpec((1, 1, 3 * D), lambda b, t: (b, 0, 0)),
        full(1, D), pl.BlockSpec((1, 3 * N_DIL * HW, D), lambda b, t: (layer, 0, 0)),
        pl.BlockSpec((HEAD_DIM // 2, TQ), lambda b, t: (0, t)),
        pl.BlockSpec((HEAD_DIM // 2, TQ), lambda b, t: (0, t)),
        full(PERM, PERM), full(PERM, PERM), full(PERM, PERM),
        pl.BlockSpec((1, D, 4 * D), lambda b, t: (layer, 0, 0)), full(1, D), full(1, D),
        full(GM_GROUPS, CHUNK, CHUNK), full(CHUNK, D), pl.BlockSpec((1, D, D), lambda b, t: (layer, 0, 0)),
    ]
    out_specs, out_shape = [], []
    for _ in range(3):
        for (_, d) in DIL_CONFIGS:
            out_specs.append(pl.BlockSpec((1, d, TQ // d, HW), lambda b, t: (b, 0, t, 0)))
            out_shape.append(jax.ShapeDtypeStruct((B, d, S // d, HW), BF))
    for keep in keeps:
        wcols = min(keep, TQ)
        first = (S - keep) // TQ if keep >= TQ else nt - 1
        out_specs.append(pl.BlockSpec(
            (1, 1, 2, HW, wcols),
            functools.partial(lambda b, t, f: (layer, b, 0, 0, jnp.maximum(t - f, 0)), f=first)))
        out_shape.append(jax.ShapeDtypeStruct((n_layers, B, 2, HW, keep), F32))
    out_specs.append(pl.BlockSpec((1, TQ, D), lambda b, t: (b, t, 0)))
    out_shape.append(jax.ShapeDtypeStruct((B, S, D), BF))
    args = [x, mod, ng, wt, cos_t, sin_t, _perm_t(1), _perm_t(4), _perm_t(16), wa, lng, lnb, wm, bias_e, wgo]
    aliases = {}
    if kv_prev is not None:
        for i, a in enumerate(kv_prev):
            aliases[len(args)] = 3 * N_DIL + i
            in_specs.append(pl.BlockSpec(memory_space=pl.ANY))
            args.append(a)
    n_alias = len(aliases)
    return pl.pallas_call(
        functools.partial(_qkv_kernel, n_alias),
        grid=(B, nt),
        in_specs=in_specs,
        out_specs=out_specs,
        out_shape=out_shape,
        input_output_aliases=aliases,
        compiler_params=_cp(("arbitrary", "arbitrary")),
        name="proj",
    )(*args)


def _band_window(n, qb):
    win = min(2 * BAND, n)
    return max(qb - 1, 0) * BAND if n > win else 0, win


def _band_bias(n, qb):
    start, win = _band_window(n, qb)
    dist = (qb * BAND + np.arange(BAND)[:, None]) - (start + np.arange(win)[None, :])
    return np.where((dist >= 0) & (dist <= BAND), 0.0, NEG).astype(np.float32)


def _band_attn_blocks(n, blocks, q_ref, k_ref, v_ref, b0_ref, b1_ref, o_ref, l_ref):
    lane = lax.broadcasted_iota(jnp.int32, (1, 2 * HEAD_DIM), 1)
    even = lane < HEAD_DIM
    pairs = [slice(p * 2 * HEAD_DIM, (p + 1) * 2 * HEAD_DIM) for p in range(HEADS // 2)]
    ss, vws = [], []
    for rr, qb in blocks:
        start, win = _band_window(n, qb)
        q = q_ref[0, rr, qb * BAND:(qb + 1) * BAND, :]
        kw = k_ref[0, rr, start:start + win, :]
        vws.append(v_ref[0, rr, start:start + win, :])
        bias = b0_ref[...] if qb == 0 else b1_ref[...]
        for sl in pairs:
            qp, kp = q[:, sl], kw[:, sl]
            for sel in (even, ~even):
                ss.append(_dot_nt(jnp.where(sel, qp, jnp.zeros_like(qp)), kp) + bias)
    ms = [jnp.max(s, axis=-1, keepdims=True) for s in ss]
    es = [jnp.exp(s - m).astype(BF) for s, m in zip(ss, ms)]
    for bi, (rr, qb) in enumerate(blocks):
        vw = vws[bi]
        rows = slice(qb * BAND, (qb + 1) * BAND)
        ones = jnp.ones((vw.shape[0], 2 * HEAD_DIM), BF)
        for p, sl in enumerate(pairs):
            i0 = bi * HEADS + 2 * p
            va = jnp.concatenate([vw[:, sl], ones], axis=1)
            r0 = _dot(es[i0], va)
            r1 = _dot(es[i0 + 1], va)
            acc = jnp.where(even, r0[:, :2 * HEAD_DIM], r1[:, :2 * HEAD_DIM])
            den = jnp.where(even, r0[:, 2 * HEAD_DIM:], r1[:, 2 * HEAD_DIM:])
            o_ref[0, rr, rows, sl] = (acc / den).astype(o_ref.dtype)
            l_ref[0, rr, rows, sl] = jnp.where(even, ms[i0], ms[i0 + 1]) + jnp.log(den)


def _band_attn_kernel(n, q_ref, k_ref, v_ref, b0_ref, b1_ref, o_ref, l_ref):
    blocks = [(rr, qb) for rr in range(q_ref.shape[1]) for qb in range(n // BAND)]
    group = ATT_GROUP if n <= BAND else 1
    for i in range(0, len(blocks), group):
        _band_attn_blocks(n, blocks[i:i + group], q_ref, k_ref, v_ref, b0_ref, b1_ref, o_ref, l_ref)


def _band_attn(q, k, v):
    B, d, n, _ = q.shape
    nqb = n // BAND
    nres = max(ATT_BLOCKS // nqb, 1)
    b0 = jnp.asarray(_band_bias(n, 0))
    b1 = jnp.asarray(_band_bias(n, min(1, nqb - 1)))
    blk = pl.BlockSpec((1, nres, n, HW), lambda b, r: (b, r, 0, 0))
    mask = pl.BlockSpec(b0.shape, lambda b, r: (0, 0))
    return pl.pallas_call(
        functools.partial(_band_attn_kernel, n),
        grid=(B, d // nres),
        in_specs=[blk, blk, blk, mask, mask],
        out_specs=[blk, blk],
        out_shape=[jax.ShapeDtypeStruct((B, d, n, HW), BF), jax.ShapeDtypeStruct((B, d, n, HW), F32)],
        compiler_params=_cp(("arbitrary", "arbitrary")),
        name=f"band_attn_d{d}",
    )(q, k, v, b0, b1)


def _final_kernel(last, x_ref, mod_ref, ng_ref, wzg_ref, ya_ref, o0_ref, l0_ref, o1_ref, l1_ref, o2_ref, l2_ref,
                  wao_ref, wo_ref, fg_ref, out_ref, no_ref, nl_ref):
    D = x_ref.shape[-1]
    wzg_ref, wao_ref, wo_ref = wzg_ref.at[0], wao_ref.at[0], wo_ref.at[0]
    mod = mod_ref[0]
    gate = mod[:, 2 * D:3 * D]
    nlt = HW // LANES
    nsub = TS // MERGE_ROWS
    subs = [slice(i * MERGE_ROWS, (i + 1) * MERGE_ROWS) for i in range(nsub)]

    def attention_rows(i):
        outs, lses = [], []
        for g, (o_ref, l_ref) in enumerate(((o0_ref, l0_ref), (o1_ref, l1_ref), (o2_ref, l2_ref))):
            d = o_ref.shape[1]
            per = MERGE_ROWS // d
            src = slice(i * per, (i + 1) * per)
            if d == 1:
                outs.append(o_ref[0, 0, src, :].astype(F32))
                lses.append(l_ref[0, 0, src, :])
                continue
            for r in range(d):
                ov = o_ref[0, r, src, :].astype(F32)
                lv = l_ref[0, r, src, :]
                for c in range(nlt):
                    dst = pl.ds(i * MERGE_ROWS + r, per, stride=d)
                    no_ref[g * nlt + c, dst, :] = ov[:, c * LANES:(c + 1) * LANES]
                    nl_ref[g * nlt + c, dst, :] = lv[:, c * LANES:(c + 1) * LANES]
            outs.append(jnp.concatenate([no_ref[g * nlt + c, subs[i], :] for c in range(nlt)], axis=1))
            lses.append(jnp.concatenate([nl_ref[g * nlt + c, subs[i], :] for c in range(nlt)], axis=1))
        mx = jnp.maximum(jnp.maximum(lses[0], lses[1]), lses[2])
        ws = [jnp.exp(l - mx) for l in lses]
        den = ws[0] + ws[1] + ws[2]
        return (ws[0] * outs[0] + ws[1] * outs[1] + ws[2] * outs[2]) / den

    hs = [_hmod(x_ref[0, sl, :], ng_ref[...], mod, D).astype(BF) for sl in subs]
    zgs = [_dot(h, wzg_ref[...]) for h in hs]
    atts = [attention_rows(i) for i in range(nsub)]
    mbs = []
    for i in range(nsub):
        yb = (atts[i] * _silu(zgs[i][:, 0:HW])).astype(BF)
        mbs.append(_dot(yb, wao_ref[...]))
    for i, sl in enumerate(subs):
        merged = ya_ref[0, sl, :].astype(F32) + _sigmoid(zgs[i][:, HW:HW + D]) * mbs[i]
        xo = x_ref[0, sl, :] + gate * _dot(merged.astype(BF), wo_ref[...])
        if last:
            xo = _rms(xo, fg_ref[...])
        out_ref[0, sl, :] = xo


def _final(layer, last, x, mod, ng, wzg, ya, ols, wao, wo, fg):
    B, S, D = x.shape
    nt = S // TS
    full = lambda *shape: pl.BlockSpec(shape, lambda b, t: (0,) * len(shape))
    tile = pl.BlockSpec((1, TS, D), lambda b, t: (b, t, 0))
    stacked = lambda *shape: pl.BlockSpec((1,) + shape, lambda b, t: (layer,) + (0,) * len(shape))
    in_specs = [tile, pl.BlockSpec((1, 1, 3 * D), lambda b, t: (b, 0, 0)), full(1, D), stacked(D, HW + D), tile]
    args = [x, mod, ng, wzg, ya]
    for (o, l), (_, d) in zip(ols, DIL_CONFIGS):
        spec = pl.BlockSpec((1, d, TS // d, HW), lambda b, t: (b, 0, t, 0))
        in_specs += [spec, spec]
        args += [o, l]
    in_specs += [stacked(HW, D), stacked(D, D), full(1, D)]
    args += [wao, wo, fg]
    return pl.pallas_call(
        functools.partial(_final_kernel, last),
        grid=(B, nt),
        in_specs=in_specs,
        out_specs=tile,
        out_shape=jax.ShapeDtypeStruct((B, S, D), F32),
        scratch_shapes=[pltpu.VMEM((N_DIL * HW // LANES, TS, LANES), F32)] * 2,
        compiler_params=_cp(("arbitrary", "arbitrary")),
        name="merge_out",
    )(*args)


def _sample_proj_kernel(x_ref, mod_ref, ng_ref, w_ref, o_ref):
    D = x_ref.shape[-1]
    h = _hmod(x_ref[...], ng_ref[...], mod_ref[...], D).astype(BF)
    o_ref[...] = _dot(h, w_ref[0].astype(BF))


def _rope_nat(x, cos_full, sin_signed):
    lane = lax.broadcasted_iota(jnp.int32, (1, LANES), 1)
    first = (lane & (HEAD_DIM - 1)) < HEAD_DIM // 2
    chunks = []
    for c in range(x.shape[1] // LANES):
        sl = slice(c * LANES, (c + 1) * LANES)
        xc = x[:, sl]
        rot = jnp.where(first, pltpu.roll(xc, LANES - HEAD_DIM // 2, 1), pltpu.roll(xc, HEAD_DIM // 2, 1))
        chunks.append(xc * cos_full[:, sl] + rot * sin_signed[:, sl])
    return jnp.concatenate(chunks, axis=1)


def _sample_mix_kernel(GW, D, p_ref, cos_ref, sin_ref, lng_ref, lnb_ref, coef_ref, bias_ref, c0_ref, c1_ref, c2_ref,
                       ya_ref, yb_ref, sga_ref, sgb_ref, kv0_ref, kv1_ref, kv2_ref, vn_ref):
    T = p_ref.shape[0]
    W3 = N_DIL * HW
    o_q = 3 * GW
    o_zb = o_q + 3 * W3
    o_ga = o_zb + HW
    o_gb = o_ga + D
    u = p_ref[:, 0:GW]
    v = p_ref[:, GW:2 * GW]
    za = p_ref[:, 2 * GW:3 * GW]
    gv = _gelu(v)
    mu = jnp.mean(gv, axis=-1, keepdims=True)
    dv = gv - mu
    var = jnp.mean(dv * dv, axis=-1, keepdims=True)
    vn = dv * lax.rsqrt(var + LN_EPS) * lng_ref[...] + lnb_ref[...]
    vn_ref[...] = vn
    vnb = vn.astype(BF).astype(F32)
    trow = lax.broadcasted_iota(jnp.int32, (T, GW), 0)
    sp = None
    for s in range(T):
        cf = jnp.where(trow >= s, coef_ref[s], 0.0).astype(BF).astype(F32)
        term = cf * vnb[s:s + 1, :]
        sp = term if sp is None else sp + term
    ya_ref[...] = _gelu(u) * (sp + bias_ref[...]) * _silu(za)
    sga_ref[...] = _sigmoid(p_ref[:, o_ga:o_ga + D])
    sgb_ref[...] = _sigmoid(p_ref[:, o_gb:o_gb + D])
    q = _rope_nat(p_ref[:, o_q:o_q + W3], cos_ref[...], sin_ref[...]) * (HEAD_DIM ** -0.5)
    k = _rope_nat(p_ref[:, o_q + W3:o_q + 2 * W3], cos_ref[...], sin_ref[...])
    val = p_ref[:, o_q + 2 * W3:o_q + 3 * W3]
    qb, kb, vb = q.astype(BF), k.astype(BF), val.astype(BF)
    lane = lax.broadcasted_iota(jnp.int32, (1, 2 * HEAD_DIM), 1)
    even = lane < HEAD_DIM
    groups = list(zip(DIL_CONFIGS, (c0_ref, c1_ref, c2_ref), (kv0_ref, kv1_ref, kv2_ref)))
    row_half = jnp.where(lax.broadcasted_iota(jnp.int32, (2 * T, 2 * HEAD_DIM), 0) >= T, 1, 0)
    lane_half = jnp.where(lax.broadcasted_iota(jnp.int32, (2 * T, 2 * HEAD_DIM), 1) >= HEAD_DIM, 1, 0)
    keep2 = row_half == lane_half
    scores = []
    for g, ((win, d), c_ref, kv_ref) in enumerate(groups):
        n = c_ref.shape[-1]
        kv_ref[:, 0:HW] = k[:, g * HW:(g + 1) * HW]
        kv_ref[:, HW:2 * HW] = val[:, g * HW:(g + 1) * HW]
        tq = lax.broadcasted_iota(jnp.int32, (2 * T, n), 0) & (T - 1)
        dist_c = n + tq - lax.broadcasted_iota(jnp.int32, (2 * T, n), 1)
        bias_c = jnp.where((dist_c <= win) & ((dist_c & (d - 1)) == 0), 0.0, NEG)
        dist_n = ((lax.broadcasted_iota(jnp.int32, (2 * T, T), 0) & (T - 1))
                  - lax.broadcasted_iota(jnp.int32, (2 * T, T), 1))
        bias_n = jnp.where((dist_n >= 0) & (dist_n <= win) & ((dist_n & (d - 1)) == 0), 0.0, NEG)
        for p in range(HEADS // 2):
            sl = slice(g * HW + p * 2 * HEAD_DIM, g * HW + (p + 1) * 2 * HEAD_DIM)
            rows = slice(p * 2 * HEAD_DIM, (p + 1) * 2 * HEAD_DIM)
            ktp = c_ref[0, 0, 0, rows, :].astype(BF)
            qp = qb[:, sl]
            q2 = jnp.concatenate([qp, qp], axis=0)
            q2 = jnp.where(keep2, q2, jnp.zeros_like(q2))
            scores.append((_dot(q2, ktp) + bias_c, _dot_nt(q2, kb[:, sl]) + bias_n))
    maxes = [jnp.maximum(jnp.max(s_c, axis=-1, keepdims=True), jnp.max(s_n, axis=-1, keepdims=True))
             for s_c, s_n in scores]
    exps = [(jnp.exp(s_c - m), jnp.exp(s_n - m)) for (s_c, s_n), m in zip(scores, maxes)]
    ms, ls, accs = [], [], []
    i = 0
    for g, ((win, d), c_ref, kv_ref) in enumerate(groups):
        mg, lg, ag = [], [], []
        for p in range(HEADS // 2):
            sl = slice(g * HW + p * 2 * HEAD_DIM, g * HW + (p + 1) * 2 * HEAD_DIM)
            rows = slice(p * 2 * HEAD_DIM, (p + 1) * 2 * HEAD_DIM)
            vtp = c_ref[0, 0, 1, rows, :].astype(BF)
            e_c, e_n = exps[i]
            l2 = jnp.sum(e_c, axis=-1, keepdims=True) + jnp.sum(e_n, axis=-1, keepdims=True)
            acc2 = _dot_nt(e_c.astype(BF), vtp) + _dot(e_n.astype(BF), vb[:, sl])
            mg.append(jnp.where(even, maxes[i][:T], maxes[i][T:]))
            lg.append(jnp.where(even, l2[:T], l2[T:]))
            ag.append(jnp.where(even, acc2[:T], acc2[T:]))
            i += 1
        ms.append(jnp.concatenate(mg, axis=1))
        ls.append(jnp.concatenate(lg, axis=1))
        accs.append(jnp.concatenate(ag, axis=1))
    mx = jnp.maximum(jnp.maximum(ms[0], ms[1]), ms[2])
    ws = [jnp.exp(m - mx) for m in ms]
    num = ws[0] * accs[0] + ws[1] * accs[1] + ws[2] * accs[2]
    den = ws[0] * ls[0] + ws[1] * ls[1] + ws[2] * ls[2]
    yb_ref[...] = (num / den) * _silu(p_ref[:, o_zb:o_zb + HW])


N_SAMPLE_IN, N_SAMPLE_OUT = 10, 8


def _attn_sample_kernel(n, GW, D, *refs):
    q_ref, k_ref, v_ref, b0_ref, b1_ref = refs[0:5]
    s_in = refs[5:5 + N_SAMPLE_IN]
    o_ref, l_ref = refs[5 + N_SAMPLE_IN:7 + N_SAMPLE_IN]
    s_out = refs[7 + N_SAMPLE_IN:]
    _band_attn_kernel(n, q_ref, k_ref, v_ref, b0_ref, b1_ref, o_ref, l_ref)
    _sample_mix_kernel(GW, D, *s_in, *s_out)


def _band_attn_sample(q, k, v, l, GW, D, T, proj, cos_s, sin_s, lng, lnb, coef, bias_e, caches):
    B, d, n, _ = q.shape
    n_tok, N = proj.shape
    DB = n_tok // T
    split = DB // B
    assert DB == B * split and d % split == 0 and len(caches) == N_DIL
    nres = d // split
    nqb = n // BAND
    b0 = jnp.asarray(_band_bias(n, 0))
    b1 = jnp.asarray(_band_bias(n, min(1, nqb - 1)))
    blk = pl.BlockSpec((1, nres, n, HW), lambda b, h: (b, h, 0, 0))
    full = lambda *shape: pl.BlockSpec(shape, lambda b, h: (0,) * len(shape))
    row = lambda w: pl.BlockSpec((T, w), lambda b, h: (b * split + h, 0))
    in_specs = [blk, blk, blk, full(*b0.shape), full(*b1.shape),
                row(N), full(T, N_DIL * HW), full(T, N_DIL * HW), full(1, GW), full(1, GW), full(T, T, GW), full(T, GW)]
    for c in caches:
        in_specs.append(pl.BlockSpec((1, 1, 2, HW, c.shape[-1]), lambda b, h: (l, b * split + h, 0, 0, 0)))
    widths = (GW, HW, D, D, 2 * HW, 2 * HW, 2 * HW, GW)
    assert len(in_specs) == 5 + N_SAMPLE_IN and len(widths) == N_SAMPLE_OUT
    outs = pl.pallas_call(
        functools.partial(_attn_sample_kernel, n, GW, D),
        grid=(B, split),
        in_specs=in_specs,
        out_specs=[blk, blk] + [row(w) for w in widths],
        out_shape=[jax.ShapeDtypeStruct((B, d, n, HW), BF), jax.ShapeDtypeStruct((B, d, n, HW), F32)]
        + [jax.ShapeDtypeStruct((n_tok, w), F32) for w in widths],
        compiler_params=_cp(("arbitrary", "arbitrary")),
        name=f"band_attn_d{d}_sample_mix",
    )(q, k, v, b0, b1, proj, cos_s, sin_s, lng, lnb, coef, bias_e, *caches)
    return outs[0:2], outs[2:]


def _attn_proj_kernel(n, q_ref, k_ref, v_ref, b0_ref, b1_ref, x_ref, mod_ref, ng_ref, w_ref, o_ref, l_ref, p_ref):
    _band_attn_kernel(n, q_ref, k_ref, v_ref, b0_ref, b1_ref, o_ref, l_ref)
    _sample_proj_kernel(x_ref, mod_ref, ng_ref, w_ref, p_ref)


def _band_attn_proj(q, k, v, l, x, mod, ng, w_in):
    B, d, n, _ = q.shape
    nrow, D = x.shape
    N = w_in.shape[-1]
    nqb = n // BAND
    nres = max(ATT_BLOCKS // nqb, 1)
    steps = B * (d // nres)
    assert N % (steps * LANES) == 0
    cols = N // steps
    b0 = jnp.asarray(_band_bias(n, 0))
    b1 = jnp.asarray(_band_bias(n, min(1, nqb - 1)))
    blk = pl.BlockSpec((1, nres, n, HW), lambda b, r: (b, r, 0, 0))
    full = lambda *shape: pl.BlockSpec(shape, lambda b, r: (0,) * len(shape))
    step = lambda b, r: b * (d // nres) + r
    outs = pl.pallas_call(
        functools.partial(_attn_proj_kernel, n),
        grid=(B, d // nres),
        in_specs=[blk, blk, blk, full(*b0.shape), full(*b1.shape), full(nrow, D), full(nrow, 3 * D), full(1, D),
                  pl.BlockSpec((1, D, cols), lambda b, r: (l, 0, step(b, r)))],
        out_specs=[blk, blk, pl.BlockSpec((nrow, cols), lambda b, r: (0, step(b, r)))],
        out_shape=[jax.ShapeDtypeStruct((B, d, n, HW), BF), jax.ShapeDtypeStruct((B, d, n, HW), F32),
                   jax.ShapeDtypeStruct((nrow, N), F32)],
        compiler_params=_cp(("arbitrary", "arbitrary")),
        name=f"band_attn_d{d}_sample_proj",
    )(q, k, v, b0, b1, x, mod, ng, w_in)
    return outs[0:2], outs[2]


def _sample_out_kernel(last, x_ref, mod_ref, ya_ref, yb_ref, sga_ref, sgb_ref, wgo_ref, wao_ref, wo_ref, fg_ref, o_ref):
    D = x_ref.shape[-1]
    wgo_ref, wao_ref, wo_ref = wgo_ref.at[0], wao_ref.at[0], wo_ref.at[0]
    merged = (sga_ref[...] * _dot(ya_ref[...].astype(BF), wgo_ref[...])
              + sgb_ref[...] * _dot(yb_ref[...].astype(BF), wao_ref[...]))
    xo = x_ref[...] + mod_ref[:, 2 * D:3 * D] * _dot(merged.astype(BF), wo_ref[...])
    if last:
        xo = _rms(xo, fg_ref[...])
    o_ref[...] = xo


def _sample_out(layer, last, x, mod, ya, yb, sga, sgb, wgo, wao, wo, fg):
    args = (x, mod, ya, yb, sga, sgb, wgo, wao, wo, fg)
    whole = lambda a: pl.BlockSpec(a.shape, lambda i, nd=a.ndim: (0,) * nd)
    stacked = lambda a: pl.BlockSpec((1,) + a.shape[1:], lambda i, nd=a.ndim: (layer,) + (0,) * (nd - 1))
    return pl.pallas_call(
        functools.partial(_sample_out_kernel, last),
        grid=(1,),
        in_specs=[whole(a) for a in args[:6]] + [stacked(a) for a in args[6:9]] + [whole(fg)],
        out_specs=pl.BlockSpec(x.shape, lambda i: (0, 0)),
        out_shape=jax.ShapeDtypeStruct(x.shape, F32),
        compiler_params=_cp(("arbitrary",)),
        name="sample_out",
    )(*args)


WCOL = 512


def _regroup_kernel(q_lo, j_zb, j_ga, j_gb, w_ref, wa_ref, wt_ref, wzg_ref):
    j = pl.program_id(1)
    blk = w_ref[0]

    @pl.when((j < q_lo) | ((j >= j_ga) & (j < j_gb)))
    def _():
        wa_ref[0] = blk.astype(BF)

    @pl.when((j >= q_lo) & (j < j_zb))
    def _():
        wt_ref[0] = jnp.transpose(blk).astype(BF)

    @pl.when((j == j_zb) | (j >= j_gb))
    def _():
        wzg_ref[0] = blk.astype(BF)


def _regroup(w_in, GW, D):
    L, _, N = w_in.shape
    W3 = N_DIL * HW
    assert GW % WCOL == 0 and D % WCOL == 0 and HW == WCOL
    n_a = 3 * GW // WCOL
    q_lo, q_hi = n_a, n_a + 3 * W3 // WCOL - 1
    j_zb = q_hi + 1
    j_ga = j_zb + 1
    n_g = D // WCOL
    j_gb = j_ga + n_g
    assert j_gb + n_g == N // WCOL and n_g == n_a // 3
    a_blk = lambda j: jnp.where(j < n_a, j, jnp.where(j < j_ga, n_a - 1, jnp.minimum(j - j_ga, n_g - 1) + n_a))
    t_blk = lambda j: jnp.clip(j - q_lo, 0, q_hi - q_lo)
    z_blk = lambda j: jnp.where(j < j_gb, 0, j - j_gb + 1)
    return pl.pallas_call(
        functools.partial(_regroup_kernel, q_lo, j_zb, j_ga, j_gb),
        grid=(L, N // WCOL),
        in_specs=[pl.BlockSpec((1, D, WCOL), lambda l, j: (l, 0, j))],
        out_specs=[
            pl.BlockSpec((1, D, WCOL), lambda l, j: (l, 0, a_blk(j))),
            pl.BlockSpec((1, WCOL, D), lambda l, j: (l, t_blk(j), 0)),
            pl.BlockSpec((1, D, WCOL), lambda l, j: (l, 0, z_blk(j))),
        ],
        out_shape=[jax.ShapeDtypeStruct((L, D, 3 * GW + D), BF), jax.ShapeDtypeStruct((L, 3 * W3, D), BF),
                   jax.ShapeDtypeStruct((L, D, HW + D), BF)],
        compiler_params=_cp(("arbitrary", "arbitrary")),
        name="regroup_w_in",
    )(w_in)


def kernel(x_prompt, x_sample, cache_kv_w128, cache_kv_w512, cache_kv_w2048, c_prompt, c_sample, w_ada, b_ada, norm_g, w_in, gm_ln_g, gm_ln_b, gm_ws, gm_bs, w_gm_out, w_att_out, w_o, final_g):
    B, S, D = x_prompt.shape
    L = w_in.shape[0]
    W3 = N_DIL * HW
    GW = gm_ln_g.shape[-1]
    assert GW == D and S % TS == 0

    c_all = jnp.concatenate([c_prompt, c_sample], axis=0)
    mod_all = _ada(c_all, w_ada, b_ada)

    o_u, o_q, o_zb, o_ga = 0, 3 * GW, 3 * GW + 3 * W3, 3 * GW + 3 * W3 + HW
    o_gb = o_ga + D
    wa_all, wqkv_t_all, wzg_all = _regroup(w_in, GW, D)
    wgo_all, wao_all, wo_all = w_gm_out.astype(BF), w_att_out.astype(BF), w_o.astype(BF)
    half = HEAD_DIM // 2
    inv_freq = ROPE_THETA ** (-jnp.arange(half, dtype=F32) / half)
    ang_p = jnp.arange(S, dtype=jnp.int32).astype(F32)[:, None] * inv_freq[None, :]
    cos_p, sin_p = jnp.cos(ang_p).T, jnp.sin(ang_p).T

    DB, T, _ = x_sample.shape
    ang_s = (PAST_LEN + jnp.arange(T, dtype=jnp.int32)).astype(F32)[:, None] * inv_freq[None, :]
    cos_s = jnp.tile(jnp.cos(ang_s), (1, 2 * N_DIL * HEADS))
    sin_s = jnp.tile(jnp.concatenate([-jnp.sin(ang_s), jnp.sin(ang_s)], axis=1), (1, N_DIL * HEADS))
    caches_t = []
    for c in (cache_kv_w128, cache_kv_w512, cache_kv_w2048):
        ct = jnp.transpose(c, (0, 1, 3, 4, 5, 2))
        caches_t.append(ct.reshape(L, DB, 2, HW, c.shape[2]))
    xs = x_sample.reshape(DB * T, D)
    kv_s = [[] for _ in DIL_CONFIGS]
    gm_v = []

    keeps = tuple(min(win, S) for win, _ in DIL_CONFIGS)
    xp = x_prompt
    kv_p = None
    for l in range(L):
        mod_p = mod_all[l, :B].reshape(B, 1, 3 * D)
        ng = norm_g[l].reshape(1, D)
        mod_s = jnp.repeat(mod_all[l, B:], T, axis=0)
        coef = jnp.repeat(jnp.transpose(gm_ws[l][:, :T, :T], (2, 1, 0)), GW // GM_GROUPS, axis=2)
        bias_s = jnp.repeat(gm_bs[l].T[:T], GW // GM_GROUPS, axis=1)
        bias_e = jnp.repeat(gm_bs[l].T, GW // GM_GROUPS, axis=1)
        outs = _qkv(xp, mod_p, ng, wqkv_t_all, cos_p, sin_p, keeps, l, L, kv_p, wa_all, gm_ln_g[l].reshape(1, GW),
                    gm_ln_b[l].reshape(1, GW), gm_ws[l], bias_e, wgo_all)
        qs, ks, vs, kv_p, ya = outs[0:3], outs[3:6], outs[6:9], outs[9:12], outs[12]
        ols = [_band_attn(qs[g], ks[g], vs[g]) for g in range(N_DIL - 2)]
        ol_mid, proj_s = _band_attn_proj(qs[-2], ks[-2], vs[-2], l, xs, mod_s, ng, w_in)
        ols.append(ol_mid)
        ol_last, so = _band_attn_sample(qs[-1], ks[-1], vs[-1], l, GW, D, T, proj_s, cos_s, sin_s,
                                        gm_ln_g[l].reshape(1, GW), gm_ln_b[l].reshape(1, GW), coef, bias_s, caches_t)
        ols.append(ol_last)
        ya_s, yb_s, sga_s, sgb_s = so[0:4]
        xs = _sample_out(l, l == L - 1, xs, mod_s, ya_s, yb_s, sga_s, sgb_s, wgo_all, wao_all, wo_all,
                         final_g.reshape(1, D))
        for g in range(N_DIL):
            kv_s[g].append(so[4 + g].reshape(DB, T, 2, HEADS, HEAD_DIM))
        gm_v.append(so[7].reshape(DB, T, GW))
        xp = _final(l, l == L - 1, xp, mod_p, ng, wzg_all, ya, ols, wao_all, wo_all, final_g.reshape(1, D))

    def kv_out(a):
        a = a.reshape(a.shape[0], a.shape[1], 2, HEADS, HEAD_DIM, a.shape[-1])
        return jnp.transpose(a, (0, 1, 5, 2, 3, 4))

    return (xp, xs.reshape(DB, T, D), kv_out(kv_p[0]), kv_out(kv_p[1]), kv_out(kv_p[2]),
            jnp.stack(kv_s[0]), jnp.stack(kv_s[1]), jnp.stack(kv_s[2]), jnp.stack(gm_v))
```

```python
import functools

import numpy as np
import jax
import jax.numpy as jnp
from jax import lax
from jax.experimental import pallas as pl
from jax.experimental.pallas import tpu as pltpu

GM_GROUPS = 8
CHUNK = 128
HEAD_DIM = 64
HEADS = 8
DIL_CONFIGS = ((128, 1), (512, 4), (2048, 16))
N_DIL = 3
HW = HEADS * HEAD_DIM
BAND = 128
PAST_LEN = 16384
ROPE_THETA = 10000.0
RMS_EPS = 1e-6
LN_EPS = 1e-5
NEG = -1e30

LANES = 128
TS = 1024
TQ = 512
PERM = 256
ATT_BLOCKS = 16
ATT_GROUP = 2
MERGE_ROWS = 256
VMEM_LIMIT = 60 * 1024 * 1024

BF = jnp.bfloat16
F32 = jnp.float32


def _cp(sem, flags=None):
    return pltpu.CompilerParams(dimension_semantics=sem, vmem_limit_bytes=VMEM_LIMIT, flags=flags)


def _gelu(x):
    return 0.5 * x * (1.0 + jnp.tanh(0.7978845608028654 * (x + 0.044715 * (x * x * x))))


def _sigmoid(x):
    return 1.0 / (1.0 + jnp.exp(-x))


def _silu(x):
    return x * _sigmoid(x)


def _rms(x, g):
    return x * lax.rsqrt(jnp.mean(x * x, axis=-1, keepdims=True) + RMS_EPS) * g


def _hmod(x, g, mod, d):
    shift = mod[:, 0:d]
    scale = mod[:, d:2 * d]
    return _rms(x, g) * (1.0 + scale) + shift


def _dot(a, b):
    return jnp.dot(a, b, preferred_element_type=F32)


def _dot_nt(a, b):
    return lax.dot_general(a, b, (((1,), (1,)), ((), ())), preferred_element_type=F32)


def _ada_kernel(c_ref, w_ref, b_ref, o_ref):
    c = c_ref[...]
    o_ref[0] = _dot(_silu(c).astype(BF), w_ref[0].astype(BF)) + b_ref[0]


def _ada(c_all, w_ada, b_ada):
    L, D, D3 = w_ada.shape
    n = c_all.shape[0]
    nt = D3 // D
    return pl.pallas_call(
        _ada_kernel,
        grid=(L, nt),
        in_specs=[
            pl.BlockSpec((n, D), lambda l, j: (0, 0)),
            pl.BlockSpec((1, D, D), lambda l, j: (l, 0, j)),
            pl.BlockSpec((1, 1, D), lambda l, j: (l, 0, j)),
        ],
        out_specs=pl.BlockSpec((1, n, D), lambda l, j: (l, 0, j)),
        out_shape=jax.ShapeDtypeStruct((L, n, D3), F32),
        compiler_params=_cp(("arbitrary", "arbitrary")),
        name="ada_mod",
    )(c_all, w_ada, b_ada.reshape(L, 1, D3))


def _branch_a_norm_v(h, wa_ref, lng_ref, lnb_ref):
    D = lng_ref.shape[-1]
    gv = _gelu(_dot(h, wa_ref[:, D:2 * D]))
    mu = jnp.mean(gv, axis=-1, keepdims=True)
    dv = gv - mu
    var = jnp.mean(dv * dv, axis=-1, keepdims=True)
    return (dv * lax.rsqrt(var + LN_EPS) * lng_ref[...] + lnb_ref[...]).astype(BF)


def _branch_a_gates(h, wa_ref):
    D = wa_ref.shape[0]
    return _gelu(_dot(h, wa_ref[:, 0:D])) * _silu(_dot(h, wa_ref[:, 2 * D:3 * D]))


def _branch_a_mix(h, vn, uz, wa_ref, wm_ref, bias_ref, wgo_ref):
    ts = h.shape[0]
    D = wgo_ref.shape[0]
    row = lax.broadcasted_iota(jnp.int32, (CHUNK, CHUNK), 0)
    col = lax.broadcasted_iota(jnp.int32, (CHUNK, CHUNK), 1)
    causal = row >= col
    nch = ts // CHUNK
    cols = []
    for g in range(GM_GROUPS):
        wmg = jnp.where(causal, wm_ref[g], 0.0).astype(BF)
        vg = jnp.concatenate([vn[c * CHUNK:(c + 1) * CHUNK, g * CHUNK:(g + 1) * CHUNK] for c in range(nch)], axis=1)
        mixed = _dot(wmg, vg)
        cols.append(jnp.concatenate([mixed[:, c * CHUNK:(c + 1) * CHUNK] for c in range(nch)], axis=0))
    sp = jnp.concatenate(cols, axis=1)
    bias = jnp.concatenate([bias_ref[...]] * (ts // CHUNK), axis=0)
    ya = (uz * (sp + bias)).astype(BF)
    ga = _dot(h, wa_ref[:, 3 * D:4 * D])
    return (_sigmoid(ga) * _dot(ya, wgo_ref[...])).astype(BF)


def _rope_t(xt, cos, sin):
    half = HEAD_DIM // 2
    pieces = []
    for h in range(xt.shape[0] // HEAD_DIM):
        x1 = xt[h * HEAD_DIM:h * HEAD_DIM + half]
        x2 = xt[h * HEAD_DIM + half:(h + 1) * HEAD_DIM]
        pieces.append(x1 * cos - x2 * sin)
        pieces.append(x2 * cos + x1 * sin)
    return jnp.concatenate(pieces, axis=0)


def _qkv_kernel(n_alias, x_ref, mod_ref, ng_ref, wt_ref, cos_ref, sin_ref, p1_ref, p4_ref, p16_ref,
                wa_ref, lng_ref, lnb_ref, wm_ref, bias_ref, wgo_ref, *refs):
    (q0_ref, q1_ref, q2_ref, k0_ref, k1_ref, k2_ref, v0_ref, v1_ref, v2_ref,
     kv0_ref, kv1_ref, kv2_ref, ya_ref) = refs[n_alias:]
    D = x_ref.shape[-1]
    wt_ref, wa_ref, wgo_ref = wt_ref.at[0], wa_ref.at[0], wgo_ref.at[0]
    h = _hmod(x_ref[0], ng_ref[...], mod_ref[0], D).astype(BF)
    w = N_DIL * HW
    cos = cos_ref[...]
    sin = sin_ref[...]
    p_refs = (p1_ref, p4_ref, p16_ref)
    dst_refs = ((q0_ref, q1_ref, q2_ref), (k0_ref, k1_ref, k2_ref), (v0_ref, v1_ref, v2_ref))
    kv_refs = (kv0_ref, kv1_ref, kv2_ref)
    chunks = [(kind, g) for g in range(N_DIL) for kind in range(3)]

    def project(kind, g):
        r0 = kind * w + g * HW
        return _dot_nt(wt_ref[r0:r0 + HW, :], h)

    vn = _branch_a_norm_v(h, wa_ref, lng_ref, lnb_ref)
    uz = _branch_a_gates(h, wa_ref)
    raw = project(*chunks[0])
    for i, (kind, g) in enumerate(chunks):
        nxt = project(*chunks[i + 1]) if i + 1 < len(chunks) else None
        if kind == 0:
            xt = _rope_t(raw, cos, sin) * (HEAD_DIM ** -0.5)
        elif kind == 1:
            xt = _rope_t(raw, cos, sin)
        else:
            xt = raw
        if kind > 0:
            kv_ref = kv_refs[g]
            kv_ref[0, 0, kind - 1] = xt[:, TQ - kv_ref.shape[-1]:]

        dst = dst_refs[kind][g]
        d = dst.shape[1]
        per = PERM // d
        if d == 1:
            dst[0, 0] = jnp.transpose(xt).astype(BF)
        else:
            xb = xt.astype(BF)
            for hf in range(TQ // PERM):
                nat = _dot_nt(p_refs[g][...], xb[:, hf * PERM:(hf + 1) * PERM]).astype(BF)
                dst[0, :, hf * per:(hf + 1) * per, :] = nat.reshape(d, per, HW)
        raw = nxt
    ya_ref[0] = _branch_a_mix(h, vn, uz, wa_ref, wm_ref, bias_ref, wgo_ref)


def _perm_t(d):
    m = np.zeros((PERM, PERM), np.float32)
    per = PERM // d
    for r in range(d):
        for i in range(per):
            m[r * per + i, i * d + r] = 1.0
    return jnp.asarray(m, dtype=BF)


def _qkv(x, mod, ng, wt, cos_t, sin_t, keeps, layer, n_layers, kv_prev, wa, lng, lnb, wm, bias_e, wgo):
    B, S, D = x.shape
    nt = S // TQ
    full = lambda *shape: pl.BlockSpec(shape, lambda b, t: (0,) * len(shape))
    in_specs = [
        pl.BlockSpec((1, TQ, D), lambda b, t: (b, t, 0)),
        pl.BlockSpec((1, 1, 3 * D), lambda b, t: (b, 0, 0)),
        full(1, D), pl.BlockSpec((1, 3 * N_DIL * HW, D), lambda b, t: (layer, 0, 0)),
        pl.BlockSpec((HEAD_DIM // 2, TQ), lambda b, t: (0, t)),
        pl.BlockSpec((HEAD_DIM // 2, TQ), lambda b, t: (0, t)),
        full(PERM, PERM), full(PERM, PERM), full(PERM, PERM),
        pl.BlockSpec((1, D, 4 * D), lambda b, t: (layer, 0, 0)), full(1, D), full(1, D),
        full(GM_GROUPS, CHUNK, CHUNK), full(CHUNK, D), pl.BlockSpec((1, D, D), lambda b, t: (layer, 0, 0)),
    ]
    out_specs, out_shape = [], []
    for _ in range(3):
        for (_, d) in DIL_CONFIGS:
            out_specs.append(pl.BlockSpec((1, d, TQ // d, HW), lambda b, t: (b, 0, t, 0)))
            out_shape.append(jax.ShapeDtypeStruct((B, d, S // d, HW), BF))
    for keep in keeps:
        wcols = min(keep, TQ)
        first = (S - keep) // TQ if keep >= TQ else nt - 1
        out_specs.append(pl.BlockSpec(
            (1, 1, 2, HW, wcols),
            functools.partial(lambda b, t, f: (layer, b, 0, 0, jnp.maximum(t - f, 0)), f=first)))
        out_shape.append(jax.ShapeDtypeStruct((n_layers, B, 2, HW, keep), F32))
    out_specs.append(pl.BlockSpec((1, TQ, D), lambda b, t: (b, t, 0)))
    out_shape.append(jax.ShapeDtypeStruct((B, S, D), BF))
    args = [x, mod, ng, wt, cos_t, sin_t, _perm_t(1), _perm_t(4), _perm_t(16), wa, lng, lnb, wm, bias_e, wgo]
    aliases = {}
    if kv_prev is not None:
        for i, a in enumerate(kv_prev):
            aliases[len(args)] = 3 * N_DIL + i
            in_specs.append(pl.BlockSpec(memory_space=pl.ANY))
            args.append(a)
    n_alias = len(aliases)
    return pl.pallas_call(
        functools.partial(_qkv_kernel, n_alias),
        grid=(B, nt),
        in_specs=in_specs,
        out_specs=out_specs,
        out_shape=out_shape,
        input_output_aliases=aliases,
        compiler_params=_cp(("arbitrary", "arbitrary")),
        name="proj",
    )(*args)


def _band_window(n, qb):
    win = min(2 * BAND, n)
    return max(qb - 1, 0) * BAND if n > win else 0, win


def _band_bias(n, qb):
    start, win = _band_window(n, qb)
    dist = (qb * BAND + np.arange(BAND)[:, None]) - (start + np.arange(win)[None, :])
    return np.where((dist >= 0) & (dist <= BAND), 0.0, NEG).astype(np.float32)


def _band_attn_blocks(n, blocks, q_ref, k_ref, v_ref, b0_ref, b1_ref, o_ref, l_ref):
    lane = lax.broadcasted_iota(jnp.int32, (1, 2 * HEAD_DIM), 1)
    even = lane < HEAD_DIM
    pairs = [slice(p * 2 * HEAD_DIM, (p + 1) * 2 * HEAD_DIM) for p in range(HEADS // 2)]
    ss, vws = [], []
    for rr, qb in blocks:
        start, win = _band_window(n, qb)
        q = q_ref[0, rr, qb * BAND:(qb + 1) * BAND, :]
        kw = k_ref[0, rr, start:start + win, :]
        vws.append(v_ref[0, rr, start:start + win, :])
        bias = b0_ref[...] if qb == 0 else b1_ref[...]
        for sl in pairs:
            qp, kp = q[:, sl], kw[:, sl]
            for sel in (even, ~even):
                ss.append(_dot_nt(jnp.where(sel, qp, jnp.zeros_like(qp)), kp) + bias)
    ms = [jnp.max(s, axis=-1, keepdims=True) for s in ss]
    es = [jnp.exp(s - m).astype(BF) for s, m in zip(ss, ms)]
    for bi, (rr, qb) in enumerate(blocks):
        vw = vws[bi]
        rows = slice(qb * BAND, (qb + 1) * BAND)
        ones = jnp.ones((vw.shape[0], 2 * HEAD_DIM), BF)
        for p, sl in enumerate(pairs):
            i0 = bi * HEADS + 2 * p
            va = jnp.concatenate([vw[:, sl], ones], axis=1)
            r0 = _dot(es[i0], va)
            r1 = _dot(es[i0 + 1], va)
            acc = jnp.where(even, r0[:, :2 * HEAD_DIM], r1[:, :2 * HEAD_DIM])
            den = jnp.where(even, r0[:, 2 * HEAD_DIM:], r1[:, 2 * HEAD_DIM:])
            o_ref[0, rr, rows, sl] = (acc / den).astype(o_ref.dtype)
            l_ref[0, rr, rows, sl] = jnp.where(even, ms[i0], ms[i0 + 1]) + jnp.log(den)


def _band_attn_kernel(n, q_ref, k_ref, v_ref, b0_ref, b1_ref, o_ref, l_ref):
    blocks = [(rr, qb) for rr in range(q_ref.shape[1]) for qb in range(n // BAND)]
    group = ATT_GROUP if n <= BAND else 1
    for i in range(0, len(blocks), group):
        _band_attn_blocks(n, blocks[i:i + group], q_ref, k_ref, v_ref, b0_ref, b1_ref, o_ref, l_ref)


def _band_attn(q, k, v):
    B, d, n, _ = q.shape
    nqb = n // BAND
    nres = max(ATT_BLOCKS // nqb, 1)
    b0 = jnp.asarray(_band_bias(n, 0))
    b1 = jnp.asarray(_band_bias(n, min(1, nqb - 1)))
    blk = pl.BlockSpec((1, nres, n, HW), lambda b, r: (b, r, 0, 0))
    mask = pl.BlockSpec(b0.shape, lambda b, r: (0, 0))
    return pl.pallas_call(
        functools.partial(_band_attn_kernel, n),
        grid=(B, d // nres),
        in_specs=[blk, blk, blk, mask, mask],
        out_specs=[blk, blk],
        out_shape=[jax.ShapeDtypeStruct((B, d, n, HW), BF), jax.ShapeDtypeStruct((B, d, n, HW), F32)],
        compiler_params=_cp(("arbitrary", "arbitrary")),
        name=f"band_attn_d{d}",
    )(q, k, v, b0, b1)


def _final_kernel(last, x_ref, mod_ref, ng_ref, wzg_ref, ya_ref, o0_ref, l0_ref, o1_ref, l1_ref, o2_ref, l2_ref,
                  wao_ref, wo_ref, fg_ref, out_ref, no_ref, nl_ref):
    D = x_ref.shape[-1]
    wzg_ref, wao_ref, wo_ref = wzg_ref.at[0], wao_ref.at[0], wo_ref.at[0]
    mod = mod_ref[0]
    gate = mod[:, 2 * D:3 * D]
    nlt = HW // LANES
    nsub = TS // MERGE_ROWS
    subs = [slice(i * MERGE_ROWS, (i + 1) * MERGE_ROWS) for i in range(nsub)]

    def attention_rows(i):
        outs, lses = [], []
        for g, (o_ref, l_ref) in enumerate(((o0_ref, l0_ref), (o1_ref, l1_ref), (o2_ref, l2_ref))):
            d = o_ref.shape[1]
            per = MERGE_ROWS // d
            src = slice(i * per, (i + 1) * per)
            base = (i % 2) * MERGE_ROWS
            half = slice(base, base + MERGE_ROWS)
            if d == 1:
                outs.append(o_ref[0, 0, src, :].astype(F32))
                lses.append(l_ref[0, 0, src, :])
                continue
            for r in range(d):
                ov = o_ref[0, r, src, :].astype(F32)
                lv = l_ref[0, r, src, :]
                for c in range(nlt):
                    dst = pl.ds(base + r, per, stride=d)
                    no_ref[g * nlt + c, dst, :] = ov[:, c * LANES:(c + 1) * LANES]
                    nl_ref[g * nlt + c, dst, :] = lv[:, c * LANES:(c + 1) * LANES]
            outs.append(jnp.concatenate([no_ref[g * nlt + c, half, :] for c in range(nlt)], axis=1))
            lses.append(jnp.concatenate([nl_ref[g * nlt + c, half, :] for c in range(nlt)], axis=1))
        mx = jnp.maximum(jnp.maximum(lses[0], lses[1]), lses[2])
        ws = [jnp.exp(l - mx) for l in lses]
        den = ws[0] + ws[1] + ws[2]
        return (ws[0] * outs[0] + ws[1] * outs[1] + ws[2] * outs[2]) / den

    hs = [_hmod(x_ref[0, sl, :], ng_ref[...], mod, D).astype(BF) for sl in subs]
    zgs = [_dot(h, wzg_ref[...]) for h in hs]
    atts = [attention_rows(i) for i in range(nsub)]
    mbs = []
    for i in range(nsub):
        yb = (atts[i] * _silu(zgs[i][:, 0:HW])).astype(BF)
        mbs.append(_dot(yb, wao_ref[...]))
    for i, sl in enumerate(subs):
        merged = ya_ref[0, sl, :].astype(F32) + _sigmoid(zgs[i][:, HW:HW + D]) * mbs[i]
        xo = x_ref[0, sl, :] + gate * _dot(merged.astype(BF), wo_ref[...])
        if last:
            xo = _rms(xo, fg_ref[...])
        out_ref[0, sl, :] = xo


def _final(layer, last, x, mod, ng, wzg, ya, ols, wao, wo, fg):
    B, S, D = x.shape
    nt = S // TS
    full = lambda *shape: pl.BlockSpec(shape, lambda b, t: (0,) * len(shape))
    tile = pl.BlockSpec((1, TS, D), lambda b, t: (b, t, 0))
    stacked = lambda *shape: pl.BlockSpec((1,) + shape, lambda b, t: (layer,) + (0,) * len(shape))
    in_specs = [tile, pl.BlockSpec((1, 1, 3 * D), lambda b, t: (b, 0, 0)), full(1, D), stacked(D, HW + D), tile]
    args = [x, mod, ng, wzg, ya]
    for (o, l), (_, d) in zip(ols, DIL_CONFIGS):
        spec = pl.BlockSpec((1, d, TS // d, HW), lambda b, t: (b, 0, t, 0))
        in_specs += [spec, spec]
        args += [o, l]
    in_specs += [stacked(HW, D), stacked(D, D), full(1, D)]
    args += [wao, wo, fg]
    return pl.pallas_call(
        functools.partial(_final_kernel, last),
        grid=(B, nt),
        in_specs=in_specs,
        out_specs=tile,
        out_shape=jax.ShapeDtypeStruct((B, S, D), F32),
        scratch_shapes=[pltpu.VMEM((N_DIL * HW // LANES, 2 * MERGE_ROWS, LANES), F32)] * 2,
        compiler_params=_cp(("arbitrary", "arbitrary")),
        name="merge_out",
    )(*args)


def _sample_proj_kernel(x_ref, mod_ref, ng_ref, w_ref, o_ref):
    D = x_ref.shape[-1]
    h = _hmod(x_ref[...], ng_ref[...], mod_ref[...], D).astype(BF)
    o_ref[...] = _dot(h, w_ref[0].astype(BF))


def _rope_nat(x, cos_full, sin_signed):
    lane = lax.broadcasted_iota(jnp.int32, (1, LANES), 1)
    first = (lane & (HEAD_DIM - 1)) < HEAD_DIM // 2
    chunks = []
    for c in range(x.shape[1] // LANES):
        sl = slice(c * LANES, (c + 1) * LANES)
        xc = x[:, sl]
        rot = jnp.where(first, pltpu.roll(xc, LANES - HEAD_DIM // 2, 1), pltpu.roll(xc, HEAD_DIM // 2, 1))
        chunks.append(xc * cos_full[:, sl] + rot * sin_signed[:, sl])
    return jnp.concatenate(chunks, axis=1)


def _sample_mix_kernel(GW, D, p_ref, cos_ref, sin_ref, lng_ref, lnb_ref, coef_ref, bias_ref, c0_ref, c1_ref, c2_ref,
                       ya_ref, yb_ref, sga_ref, sgb_ref, kv0_ref, kv1_ref, kv2_ref, vn_ref):
    T = p_ref.shape[0]
    W3 = N_DIL * HW
    o_q = 3 * GW
    o_zb = o_q + 3 * W3
    o_ga = o_zb + HW
    o_gb = o_ga + D
    u = p_ref[:, 0:GW]
    v = p_ref[:, GW:2 * GW]
    za = p_ref[:, 2 * GW:3 * GW]
    gv = _gelu(v)
    mu = jnp.mean(gv, axis=-1, keepdims=True)
    dv = gv - mu
    var = jnp.mean(dv * dv, axis=-1, keepdims=True)
    vn = dv * lax.rsqrt(var + LN_EPS) * lng_ref[...] + lnb_ref[...]
    vn_ref[...] = vn
    vnb = vn.astype(BF).astype(F32)
    trow = lax.broadcasted_iota(jnp.int32, (T, GW), 0)
    sp = None
    for s in range(T):
        cf = jnp.where(trow >= s, coef_ref[s], 0.0).astype(BF).astype(F32)
        term = cf * vnb[s:s + 1, :]
        sp = term if sp is None else sp + term
    ya_ref[...] = _gelu(u) * (sp + bias_ref[...]) * _silu(za)
    sga_ref[...] = _sigmoid(p_ref[:, o_ga:o_ga + D])
    sgb_ref[...] = _sigmoid(p_ref[:, o_gb:o_gb + D])
    q = _rope_nat(p_ref[:, o_q:o_q + W3], cos_ref[...], sin_ref[...]) * (HEAD_DIM ** -0.5)
    k = _rope_nat(p_ref[:, o_q + W3:o_q + 2 * W3], cos_ref[...], sin_ref[...])
    val = p_ref[:, o_q + 2 * W3:o_q + 3 * W3]
    qb, kb, vb = q.astype(BF), k.astype(BF), val.astype(BF)
    lane = lax.broadcasted_iota(jnp.int32, (1, 2 * HEAD_DIM), 1)
    even = lane < HEAD_DIM
    groups = list(zip(DIL_CONFIGS, (c0_ref, c1_ref, c2_ref), (kv0_ref, kv1_ref, kv2_ref)))
    row_half = jnp.where(lax.broadcasted_iota(jnp.int32, (2 * T, 2 * HEAD_DIM), 0) >= T, 1, 0)
    lane_half = jnp.where(lax.broadcasted_iota(jnp.int32, (2 * T, 2 * HEAD_DIM), 1) >= HEAD_DIM, 1, 0)
    keep2 = row_half == lane_half
    scores = []
    for g, ((win, d), c_ref, kv_ref) in enumerate(groups):
        n = c_ref.shape[-1]
        kv_ref[:, 0:HW] = k[:, g * HW:(g + 1) * HW]
        kv_ref[:, HW:2 * HW] = val[:, g * HW:(g + 1) * HW]
        tq = lax.broadcasted_iota(jnp.int32, (2 * T, n), 0) & (T - 1)
        dist_c = n + tq - lax.broadcasted_iota(jnp.int32, (2 * T, n), 1)
        bias_c = jnp.where((dist_c <= win) & ((dist_c & (d - 1)) == 0), 0.0, NEG)
        dist_n = ((lax.broadcasted_iota(jnp.int32, (2 * T, T), 0) & (T - 1))
                  - lax.broadcasted_iota(jnp.int32, (2 * T, T), 1))
        bias_n = jnp.where((dist_n >= 0) & (dist_n <= win) & ((dist_n & (d - 1)) == 0), 0.0, NEG)
        for p in range(HEADS // 2):
            sl = slice(g * HW + p * 2 * HEAD_DIM, g * HW + (p + 1) * 2 * HEAD_DIM)
            rows = slice(p * 2 * HEAD_DIM, (p + 1) * 2 * HEAD_DIM)
            ktp = c_ref[0, 0, 0, rows, :].astype(BF)
            qp = qb[:, sl]
            q2 = jnp.concatenate([qp, qp], axis=0)
            q2 = jnp.where(keep2, q2, jnp.zeros_like(q2))
            scores.append((_dot(q2, ktp) + bias_c, _dot_nt(q2, kb[:, sl]) + bias_n))
    maxes = [jnp.maximum(jnp.max(s_c, axis=-1, keepdims=True), jnp.max(s_n, axis=-1, keepdims=True))
             for s_c, s_n in scores]
    exps = [(jnp.exp(s_c - m), jnp.exp(s_n - m)) for (s_c, s_n), m in zip(scores, maxes)]
    ms, ls, accs = [], [], []
    i = 0
    for g, ((win, d), c_ref, kv_ref) in enumerate(groups):
        mg, lg, ag = [], [], []
        for p in range(HEADS // 2):
            sl = slice(g * HW + p * 2 * HEAD_DIM, g * HW + (p + 1) * 2 * HEAD_DIM)
            rows = slice(p * 2 * HEAD_DIM, (p + 1) * 2 * HEAD_DIM)
            vtp = c_ref[0, 0, 1, rows, :].astype(BF)
            e_c, e_n = exps[i]
            l2 = jnp.sum(e_c, axis=-1, keepdims=True) + jnp.sum(e_n, axis=-1, keepdims=True)
            acc2 = _dot_nt(e_c.astype(BF), vtp) + _dot(e_n.astype(BF), vb[:, sl])
            mg.append(jnp.where(even, maxes[i][:T], maxes[i][T:]))
            lg.append(jnp.where(even, l2[:T], l2[T:]))
            ag.append(jnp.where(even, acc2[:T], acc2[T:]))
            i += 1
        ms.append(jnp.concatenate(mg, axis=1))
        ls.append(jnp.concatenate(lg, axis=1))
        accs.append(jnp.concatenate(ag, axis=1))
    mx = jnp.maximum(jnp.maximum(ms[0], ms[1]), ms[2])
    ws = [jnp.exp(m - mx) for m in ms]
    num = ws[0] * accs[0] + ws[1] * accs[1] + ws[2] * accs[2]
    den = ws[0] * ls[0] + ws[1] * ls[1] + ws[2] * ls[2]
    yb_ref[...] = (num / den) * _silu(p_ref[:, o_zb:o_zb + HW])


N_SAMPLE_IN, N_SAMPLE_OUT = 10, 8


def _attn_sample_kernel(n, GW, D, *refs):
    q_ref, k_ref, v_ref, b0_ref, b1_ref = refs[0:5]
    s_in = refs[5:5 + N_SAMPLE_IN]
    o_ref, l_ref = refs[5 + N_SAMPLE_IN:7 + N_SAMPLE_IN]
    s_out = refs[7 + N_SAMPLE_IN:]
    _band_attn_kernel(n, q_ref, k_ref, v_ref, b0_ref, b1_ref, o_ref, l_ref)
    _sample_mix_kernel(GW, D, *s_in, *s_out)


def _band_attn_sample(q, k, v, l, GW, D, T, proj, cos_s, sin_s, lng, lnb, coef, bias_e, caches):
    B, d, n, _ = q.shape
    n_tok, N = proj.shape
    DB = n_tok // T
    split = DB // B
    assert DB == B * split and d % split == 0 and len(caches) == N_DIL
    nres = d // split
    nqb = n // BAND
    b0 = jnp.asarray(_band_bias(n, 0))
    b1 = jnp.asarray(_band_bias(n, min(1, nqb - 1)))
    blk = pl.BlockSpec((1, nres, n, HW), lambda b, h: (b, h, 0, 0))
    full = lambda *shape: pl.BlockSpec(shape, lambda b, h: (0,) * len(shape))
    row = lambda w: pl.BlockSpec((T, w), lambda b, h: (b * split + h, 0))
    in_specs = [blk, blk, blk, full(*b0.shape), full(*b1.shape),
                row(N), full(T, N_DIL * HW), full(T, N_DIL * HW), full(1, GW), full(1, GW), full(T, T, GW), full(T, GW)]
    for c in caches:
        in_specs.append(pl.BlockSpec((1, 1, 2, HW, c.shape[-1]), lambda b, h: (l, b * split + h, 0, 0, 0)))
    widths = (GW, HW, D, D, 2 * HW, 2 * HW, 2 * HW, GW)
    assert len(in_specs) == 5 + N_SAMPLE_IN and len(widths) == N_SAMPLE_OUT
    outs = pl.pallas_call(
        functools.partial(_attn_sample_kernel, n, GW, D),
        grid=(B, split),
        in_specs=in_specs,
        out_specs=[blk, blk] + [row(w) for w in widths],
        out_shape=[jax.ShapeDtypeStruct((B, d, n, HW), BF), jax.ShapeDtypeStruct((B, d, n, HW), F32)]
        + [jax.ShapeDtypeStruct((n_tok, w), F32) for w in widths],
        compiler_params=_cp(("arbitrary", "arbitrary")),
        name=f"band_attn_d{d}_sample_mix",
    )(q, k, v, b0, b1, proj, cos_s, sin_s, lng, lnb, coef, bias_e, *caches)
    return outs[0:2], outs[2:]


def _attn_proj_kernel(n, q_ref, k_ref, v_ref, b0_ref, b1_ref, x_ref, mod_ref, ng_ref, w_ref, o_ref, l_ref, p_ref):
    _band_attn_kernel(n, q_ref, k_ref, v_ref, b0_ref, b1_ref, o_ref, l_ref)
    _sample_proj_kernel(x_ref, mod_ref, ng_ref, w_ref, p_ref)


def _band_attn_proj(q, k, v, l, x, mod, ng, w_in):
    B, d, n, _ = q.shape
    nrow, D = x.shape
    N = w_in.shape[-1]
    nqb = n // BAND
    nres = max(ATT_BLOCKS // nqb, 1)
    steps = B * (d // nres)
    assert N % (steps * LANES) == 0
    cols = N // steps
    b0 = jnp.asarray(_band_bias(n, 0))
    b1 = jnp.asarray(_band_bias(n, min(1, nqb - 1)))
    blk = pl.BlockSpec((1, nres, n, HW), lambda b, r: (b, r, 0, 0))
    full = lambda *shape: pl.BlockSpec(shape, lambda b, r: (0,) * len(shape))
    step = lambda b, r: b * (d // nres) + r
    outs = pl.pallas_call(
        functools.partial(_attn_proj_kernel, n),
        grid=(B, d // nres),
        in_specs=[blk, blk, blk, full(*b0.shape), full(*b1.shape), full(nrow, D), full(nrow, 3 * D), full(1, D),
                  pl.BlockSpec((1, D, cols), lambda b, r: (l, 0, step(b, r)))],
        out_specs=[blk, blk, pl.BlockSpec((nrow, cols), lambda b, r: (0, step(b, r)))],
        out_shape=[jax.ShapeDtypeStruct((B, d, n, HW), BF), jax.ShapeDtypeStruct((B, d, n, HW), F32),
                   jax.ShapeDtypeStruct((nrow, N), F32)],
        compiler_params=_cp(("arbitrary", "arbitrary")),
        name=f"band_attn_d{d}_sample_proj",
    )(q, k, v, b0, b1, x, mod, ng, w_in)
    return outs[0:2], outs[2]


def _sample_out_kernel(last, x_ref, mod_ref, ya_ref, yb_ref, sga_ref, sgb_ref, wgo_ref, wao_ref, wo_ref, fg_ref, o_ref):
    D = x_ref.shape[-1]
    wgo_ref, wao_ref, wo_ref = wgo_ref.at[0], wao_ref.at[0], wo_ref.at[0]
    merged = (sga_ref[...] * _dot(ya_ref[...].astype(BF), wgo_ref[...])
              + sgb_ref[...] * _dot(yb_ref[...].astype(BF), wao_ref[...]))
    xo = x_ref[...] + mod_ref[:, 2 * D:3 * D] * _dot(merged.astype(BF), wo_ref[...])
    if last:
        xo = _rms(xo, fg_ref[...])
    o_ref[...] = xo


def _sample_out(layer, last, x, mod, ya, yb, sga, sgb, wgo, wao, wo, fg):
    args = (x, mod, ya, yb, sga, sgb, wgo, wao, wo, fg)
    whole = lambda a: pl.BlockSpec(a.shape, lambda i, nd=a.ndim: (0,) * nd)
    stacked = lambda a: pl.BlockSpec((1,) + a.shape[1:], lambda i, nd=a.ndim: (layer,) + (0,) * (nd - 1))
    return pl.pallas_call(
        functools.partial(_sample_out_kernel, last),
        grid=(1,),
        in_specs=[whole(a) for a in args[:6]] + [stacked(a) for a in args[6:9]] + [whole(fg)],
        out_specs=pl.BlockSpec(x.shape, lambda i: (0, 0)),
        out_shape=jax.ShapeDtypeStruct(x.shape, F32),
        compiler_params=_cp(("arbitrary",)),
        name="sample_out",
    )(*args)


WCOL = 512


def _regroup_kernel(q_lo, j_zb, j_ga, j_gb, w_ref, wa_ref, wt_ref, wzg_ref):
    j = pl.program_id(1)
    blk = w_ref[0]

    @pl.when((j < q_lo) | ((j >= j_ga) & (j < j_gb)))
    def _():
        wa_ref[0] = blk.astype(BF)

    @pl.when((j >= q_lo) & (j < j_zb))
    def _():
        wt_ref[0] = jnp.transpose(blk).astype(BF)

    @pl.when((j == j_zb) | (j >= j_gb))
    def _():
        wzg_ref[0] = blk.astype(BF)


def _regroup(w_in, GW, D):
    L, _, N = w_in.shape
    W3 = N_DIL * HW
    assert GW % WCOL == 0 and D % WCOL == 0 and HW == WCOL
    n_a = 3 * GW // WCOL
    q_lo, q_hi = n_a, n_a + 3 * W3 // WCOL - 1
    j_zb = q_hi + 1
    j_ga = j_zb + 1
    n_g = D // WCOL
    j_gb = j_ga + n_g
    assert j_gb + n_g == N // WCOL and n_g == n_a // 3
    a_blk = lambda j: jnp.where(j < n_a, j, jnp.where(j < j_ga, n_a - 1, jnp.minimum(j - j_ga, n_g - 1) + n_a))
    t_blk = lambda j: jnp.clip(j - q_lo, 0, q_hi - q_lo)
    z_blk = lambda j: jnp.where(j < j_gb, 0, j - j_gb + 1)
    return pl.pallas_call(
        functools.partial(_regroup_kernel, q_lo, j_zb, j_ga, j_gb),
        grid=(L, N // WCOL),
        in_specs=[pl.BlockSpec((1, D, WCOL), lambda l, j: (l, 0, j))],
        out_specs=[
            pl.BlockSpec((1, D, WCOL), lambda l, j: (l, 0, a_blk(j))),
            pl.BlockSpec((1, WCOL, D), lambda l, j: (l, t_blk(j), 0)),
            pl.BlockSpec((1, D, WCOL), lambda l, j: (l, 0, z_blk(j))),
        ],
        out_shape=[jax.ShapeDtypeStruct((L, D, 3 * GW + D), BF), jax.ShapeDtypeStruct((L, 3 * W3, D), BF),
                   jax.ShapeDtypeStruct((L, D, HW + D), BF)],
        compiler_params=_cp(("arbitrary", "arbitrary")),
        name="regroup_w_in",
    )(w_in)


def kernel(x_prompt, x_sample, cache_kv_w128, cache_kv_w512, cache_kv_w2048, c_prompt, c_sample, w_ada, b_ada, norm_g, w_in, gm_ln_g, gm_ln_b, gm_ws, gm_bs, w_gm_out, w_att_out, w_o, final_g):
    B, S, D = x_prompt.shape
    L = w_in.shape[0]
    W3 = N_DIL * HW
    GW = gm_ln_g.shape[-1]
    assert GW == D and S % TS == 0

    c_all = jnp.concatenate([c_prompt, c_sample], axis=0)
    mod_all = _ada(c_all, w_ada, b_ada)

    o_u, o_q, o_zb, o_ga = 0, 3 * GW, 3 * GW + 3 * W3, 3 * GW + 3 * W3 + HW
    o_gb = o_ga + D
    wa_all, wqkv_t_all, wzg_all = _regroup(w_in, GW, D)
    wgo_all, wao_all, wo_all = w_gm_out.astype(BF), w_att_out.astype(BF), w_o.astype(BF)
    half = HEAD_DIM // 2
    inv_freq = ROPE_THETA ** (-jnp.arange(half, dtype=F32) / half)
    ang_p = jnp.arange(S, dtype=jnp.int32).astype(F32)[:, None] * inv_freq[None, :]
    cos_p, sin_p = jnp.cos(ang_p).T, jnp.sin(ang_p).T

    DB, T, _ = x_sample.shape
    ang_s = (PAST_LEN + jnp.arange(T, dtype=jnp.int32)).astype(F32)[:, None] * inv_freq[None, :]
    cos_s = jnp.tile(jnp.cos(ang_s), (1, 2 * N_DIL * HEADS))
    sin_s = jnp.tile(jnp.concatenate([-jnp.sin(ang_s), jnp.sin(ang_s)], axis=1), (1, N_DIL * HEADS))
    caches_t = []
    for c in (cache_kv_w128, cache_kv_w512, cache_kv_w2048):
        ct = jnp.transpose(c, (0, 1, 3, 4, 5, 2))
        caches_t.append(ct.reshape(L, DB, 2, HW, c.shape[2]))
    xs = x_sample.reshape(DB * T, D)
    kv_s = [[] for _ in DIL_CONFIGS]
    gm_v = []

    keeps = tuple(min(win, S) for win, _ in DIL_CONFIGS)
    xp = x_prompt
    kv_p = None
    for l in range(L):
        mod_p = mod_all[l, :B].reshape(B, 1, 3 * D)
        ng = norm_g[l].reshape(1, D)
        mod_s = jnp.repeat(mod_all[l, B:], T, axis=0)
        coef = jnp.repeat(jnp.transpose(gm_ws[l][:, :T, :T], (2, 1, 0)), GW // GM_GROUPS, axis=2)
        bias_s = jnp.repeat(gm_bs[l].T[:T], GW // GM_GROUPS, axis=1)
        bias_e = jnp.repeat(gm_bs[l].T, GW // GM_GROUPS, axis=1)
        outs = _qkv(xp, mod_p, ng, wqkv_t_all, cos_p, sin_p, keeps, l, L, kv_p, wa_all, gm_ln_g[l].reshape(1, GW),
                    gm_ln_b[l].reshape(1, GW), gm_ws[l], bias_e, wgo_all)
        qs, ks, vs, kv_p, ya = outs[0:3], outs[3:6], outs[6:9], outs[9:12], outs[12]
        ols = [_band_attn(qs[g], ks[g], vs[g]) for g in range(N_DIL - 2)]
        ol_mid, proj_s = _band_attn_proj(qs[-2], ks[-2], vs[-2], l, xs, mod_s, ng, w_in)
        ols.append(ol_mid)
        ol_last, so = _band_attn_sample(qs[-1], ks[-1], vs[-1], l, GW, D, T, proj_s, cos_s, sin_s,
                                        gm_ln_g[l].reshape(1, GW), gm_ln_b[l].reshape(1, GW), coef, bias_s, caches_t)
        ols.append(ol_last)
        ya_s, yb_s, sga_s, sgb_s = so[0:4]
        xs = _sample_out(l, l == L - 1, xs, mod_s, ya_s, yb_s, sga_s, sgb_s, wgo_all, wao_all, wo_all,
                         final_g.reshape(1, D))
        for g in range(N_DIL):
            kv_s[g].append(so[4 + g].reshape(DB, T, 2, HEADS, HEAD_DIM))
        gm_v.append(so[7].reshape(DB, T, GW))
        xp = _final(l, l == L - 1, xp, mod_p, ng, wzg_all, ya, ols, wao_all, wo_all, final_g.reshape(1, D))

    def kv_out(a):
        a = a.reshape(a.shape[0], a.shape[1], 2, HEADS, HEAD_DIM, a.shape[-1])
        return jnp.transpose(a, (0, 1, 5, 2, 3, 4))

    return (xp, xs.reshape(DB, T, D), kv_out(kv_p[0]), kv_out(kv_p[1]), kv_out(kv_p[2]),
            jnp.stack(kv_s[0]), jnp.stack(kv_s[1]), jnp.stack(kv_s[2]), jnp.stack(gm_v))
```
